```python
import jax, jax.numpy as jnp
from jax import lax
import numpy as np

D_MODEL = 1024
BATCH = 32
SEQ = 2048
DEPTH = 2
DEC_BATCH = 32
DEC_SEQ = 32
PAST_LEN = 2048

CHUNK = 64
HEAD_DIM = 64
N_HEADS_A = D_MODEL // HEAD_DIM // 2
N_HEADS_B = D_MODEL // HEAD_DIM // 2
N_HEADS_C = D_MODEL // HEAD_DIM
A_LEFT_CHUNKS = 8
A_PAST = A_LEFT_CHUNKS * CHUNK
A_BAND = A_PAST + CHUNK
REL_CLIP = 128
Q_BLOCK = 128
N_MEM = 256
N_HEADS_X = 4
HEAD_DIM_X = D_MODEL // N_HEADS_X
D_FF = 4 * D_MODEL
N_AB = (DEPTH + 1) // 2
N_SB = DEPTH // 2
FORGET_BIAS_INIT = 3.0
EPS = 1e-6
NEG = -1e30

kernel_name = 'hybrid_chunk_stream_encoder_step'


def rmsnorm(x, g):
    xf = x.astype(jnp.float32)
    y = xf * lax.rsqrt(jnp.mean(xf * xf, axis=-1, keepdims=True) + EPS)
    return (y * g.astype(jnp.float32)).astype(x.dtype)


def band_gather(t):
    b, s, h, d = t.shape
    nc = s // CHUNK
    tp = jnp.pad(t, ((0, 0), (A_PAST, 0), (0, 0), (0, 0)))
    tc = tp.reshape(b, nc + A_LEFT_CHUNKS, CHUNK, h, d)
    band = jnp.stack([tc[:, i:i + nc] for i in range(A_LEFT_CHUNKS + 1)], axis=2)
    return band.reshape(b, nc, A_BAND, h, d)


def chunk_band_attend(q, k, v, q_pos, k_pos, rel_bias):
    s = jnp.einsum('bnqhd,bnkhd->bnhqk', q, k, preferred_element_type=jnp.float32) * (HEAD_DIM ** -0.5)
    rel = jnp.clip(k_pos[:, None, :] - q_pos[:, :, None], -REL_CLIP, REL_CLIP) + REL_CLIP
    bias = jnp.transpose(rel_bias[rel], (0, 3, 1, 2)).astype(jnp.float32)
    qc = q_pos // CHUNK
    kc = k_pos // CHUNK
    ok = ((k_pos[:, None, :] >= 0) & (kc[:, None, :] <= qc[:, :, None])
          & (kc[:, None, :] >= qc[:, :, None] - A_LEFT_CHUNKS))
    s = jnp.where(ok[None, :, None], s + bias[None], NEG)
    p = jax.nn.softmax(s, axis=-1).astype(v.dtype)
    return jnp.einsum('bnhqk,bnkhd->bnqhd', p, v)


def forget_attend(q, k, v, cq, ck, q_pos, k_pos):
    s = jnp.einsum('bqhd,bkhd->bhqk', q, k, preferred_element_type=jnp.float32) * (HEAD_DIM ** -0.5)
    decay = jnp.transpose(cq, (0, 2, 1))[:, :, :, None] - jnp.transpose(ck, (0, 2, 1))[:, :, None, :]
    ok = k_pos[None, :] <= q_pos[:, None]
    s = jnp.where(ok, s + decay.astype(jnp.float32), NEG)
    p = jax.nn.softmax(s, axis=-1).astype(v.dtype)
    return jnp.einsum('bhqk,bkhd->bqhd', p, v)


def stick_breaking_attend(q, k, v, q_pos, k_pos):
    z = jnp.einsum('bqhd,bkhd->bhqk', q, k, preferred_element_type=jnp.float32) * (HEAD_DIM ** -0.5)
    before = k_pos[None, :] < q_pos[:, None]
    log_keep = jnp.where(before, jax.nn.log_sigmoid(-z), 0.0)
    tail = lax.cumsum(log_keep, axis=3, reverse=True) - log_keep
    w = jnp.where(before, jnp.exp(jax.nn.log_sigmoid(z) + tail), 0.0)
    return jnp.einsum('bhqk,bkhd->bqhd', w.astype(v.dtype), v)


def ab_project(xn, w_in, b_f, qk_gain):
    b, t, _ = xn.shape
    da = N_HEADS_A * HEAD_DIM
    db = N_HEADS_B * HEAD_DIM
    cuts = [da, 2 * da, 3 * da, 3 * da + db, 3 * da + 2 * db, 3 * da + 3 * db]
    qa, ka, va, qb, kb, vb, fl = jnp.split(xn @ w_in, cuts, axis=-1)
    qa = rmsnorm(qa.reshape(b, t, N_HEADS_A, HEAD_DIM), qk_gain[0])
    ka = rmsnorm(ka.reshape(b, t, N_HEADS_A, HEAD_DIM), qk_gain[1])
    va = va.reshape(b, t, N_HEADS_A, HEAD_DIM)
    qb = rmsnorm(qb.reshape(b, t, N_HEADS_B, HEAD_DIM), qk_gain[2])
    kb = rmsnorm(kb.reshape(b, t, N_HEADS_B, HEAD_DIM), qk_gain[3])
    vb = vb.reshape(b, t, N_HEADS_B, HEAD_DIM)
    logf = jax.nn.log_sigmoid((fl + b_f).astype(jnp.float32))
    return qa, ka, va, qb, kb, vb, logf


def ab_merge(oa, ob, w_out):
    b, t = oa.shape[:2]
    return jnp.concatenate([oa.reshape(b, t, -1), ob.reshape(b, t, -1)], axis=-1) @ w_out


def ab_prompt(xn, w_in, b_f, qk_gain, rel_bias, w_out):
    b, s, _ = xn.shape
    qa, ka, va, qb, kb, vb, logf = ab_project(xn, w_in, b_f, qk_gain)
    nc = s // CHUNK
    starts = jnp.arange(nc)[:, None] * CHUNK
    q_pos = starts + jnp.arange(CHUNK)[None, :]
    k_pos = starts - A_PAST + jnp.arange(A_BAND)[None, :]
    oa = chunk_band_attend(qa.reshape(b, nc, CHUNK, N_HEADS_A, HEAD_DIM), band_gather(ka),
                           band_gather(va), q_pos, k_pos, rel_bias)
    oa = oa.reshape(b, s, N_HEADS_A, HEAD_DIM)
    c = jnp.cumsum(logf, axis=1)
    pos = jnp.arange(s)
    blocks = []
    for j in range(s // Q_BLOCK):
        lo, hi = j * Q_BLOCK, (j + 1) * Q_BLOCK
        blocks.append(forget_attend(qb[:, lo:hi], kb[:, :hi], vb[:, :hi], c[:, lo:hi], c[:, :hi],
                                    pos[lo:hi], pos[:hi]))
    ob = jnp.concatenate(blocks, axis=1)
    keep = min(A_PAST, s)
    return ab_merge(oa, ob, w_out), (ka[:, s - keep:], va[:, s - keep:], kb, vb, logf)


def ab_sample(xn, ca_k, ca_v, cb_k, cb_v, cb_logf, w_in, b_f, qk_gain, rel_bias, w_out):
    b, t, _ = xn.shape
    qa, ka, va, qb, kb, vb, logf = ab_project(xn, w_in, b_f, qk_gain)
    n_band = ca_k.shape[1]
    past = cb_k.shape[1]
    q_pos = past + jnp.arange(t)
    ka_all = jnp.concatenate([ca_k, ka], axis=1)
    va_all = jnp.concatenate([ca_v, va], axis=1)
    k_pos = past - n_band + jnp.arange(n_band + t)
    oa = chunk_band_attend(qa[:, None], ka_all[:, None], va_all[:, None], q_pos[None], k_pos[None],
                           rel_bias)[:, 0]
    kb_all = jnp.concatenate([cb_k, kb], axis=1)
    vb_all = jnp.concatenate([cb_v, vb], axis=1)
    c = jnp.cumsum(jnp.concatenate([cb_logf.astype(jnp.float32), logf], axis=1), axis=1)
    ob = forget_attend(qb, kb_all, vb_all, c[:, past:], c, q_pos, jnp.arange(past + t))
    return ab_merge(oa, ob, w_out), (ka, va, kb, vb, logf)


def sb_project(xn, w_in):
    b, t, _ = xn.shape
    q, k, v = jnp.split(xn @ w_in, 3, axis=-1)
    shp = (b, t, N_HEADS_C, HEAD_DIM)
    return q.reshape(shp), k.reshape(shp), v.reshape(shp)


def sb_prompt(xn, w_in, w_out):
    b, s, _ = xn.shape
    q, k, v = sb_project(xn, w_in)
    pos = jnp.arange(s)
    blocks = []
    for j in range(s // Q_BLOCK):
        lo, hi = j * Q_BLOCK, (j + 1) * Q_BLOCK
        blocks.append(stick_breaking_attend(q[:, lo:hi], k[:, :hi], v[:, :hi], pos[lo:hi], pos[:hi]))
    o = jnp.concatenate(blocks, axis=1)
    return o.reshape(b, s, -1) @ w_out, (k, v)


def sb_sample(xn, cc_k, cc_v, w_in, w_out):
    b, t, _ = xn.shape
    q, k, v = sb_project(xn, w_in)
    past = cc_k.shape[1]
    k_all = jnp.concatenate([cc_k, k], axis=1)
    v_all = jnp.concatenate([cc_v, v], axis=1)
    o = stick_breaking_attend(q, k_all, v_all, past + jnp.arange(t), jnp.arange(past + t))
    return o.reshape(b, t, -1) @ w_out, (k, v)


def mem_kv(mem, g_mem, w_kv, k_gain):
    b, n, _ = mem.shape
    k, v = jnp.split(rmsnorm(mem, g_mem) @ w_kv, 2, axis=-1)
    k = rmsnorm(k.reshape(b, n, N_HEADS_X, HEAD_DIM_X), k_gain)
    return k, v.reshape(b, n, N_HEADS_X, HEAD_DIM_X)


def cross_attend(hn, k, v, w_q, q_gain, w_o):
    b, t, _ = hn.shape
    q = rmsnorm((hn @ w_q).reshape(b, t, N_HEADS_X, HEAD_DIM_X), q_gain)
    s = jnp.einsum('bqhd,bkhd->bhqk', q, k, preferred_element_type=jnp.float32) * (HEAD_DIM_X ** -0.5)
    p = jax.nn.softmax(s, axis=-1).astype(v.dtype)
    o = jnp.einsum('bhqk,bkhd->bqhd', p, v)
    return o.reshape(b, t, -1) @ w_o


def mlp(hn, w_up, w_down):
    return jnp.square(jax.nn.relu(hn @ w_up)) @ w_down


def setup_inputs(seed: int = 0) -> dict:
    key = jax.random.key(seed)
    keys = iter(jax.random.split(key, 64))

    def nrm(shape, scale=1.0):
        return jax.random.normal(next(keys), shape, jnp.float32) * scale

    def gain(shape):
        return 1.0 + 0.02 * nrm(shape)

    d = D_MODEL
    dab = (N_HEADS_A + N_HEADS_B) * HEAD_DIM
    dc = N_HEADS_C * HEAD_DIM
    dx = N_HEADS_X * HEAD_DIM_X
    a_rows = min(A_PAST, PAST_LEN)
    return {
        'x_prompt': nrm((BATCH, SEQ, d)),
        'x_sample': nrm((DEC_BATCH, DEC_SEQ, d)),
        'mem_prompt': nrm((BATCH, N_MEM, d)),
        'cache_a_k': nrm((N_AB, DEC_BATCH, a_rows, N_HEADS_A, HEAD_DIM)),
        'cache_a_v': nrm((N_AB, DEC_BATCH, a_rows, N_HEADS_A, HEAD_DIM)),
        'cache_b_k': nrm((N_AB, DEC_BATCH, PAST_LEN, N_HEADS_B, HEAD_DIM)),
        'cache_b_v': nrm((N_AB, DEC_BATCH, PAST_LEN, N_HEADS_B, HEAD_DIM)),
        'cache_b_logf': jax.nn.log_sigmoid(FORGET_BIAS_INIT + nrm((N_AB, DEC_BATCH, PAST_LEN, N_HEADS_B))),
        'cache_c_k': nrm((N_SB, DEC_BATCH, PAST_LEN, N_HEADS_C, HEAD_DIM)),
        'cache_c_v': nrm((N_SB, DEC_BATCH, PAST_LEN, N_HEADS_C, HEAD_DIM)),
        'cache_mem_k': nrm((DEPTH, DEC_BATCH, N_MEM, N_HEADS_X, HEAD_DIM_X)),
        'cache_mem_v': nrm((DEPTH, DEC_BATCH, N_MEM, N_HEADS_X, HEAD_DIM_X)),
        'norm_mix': gain((DEPTH, d)),
        'norm_cross': gain((DEPTH, d)),
        'norm_mlp': gain((DEPTH, d)),
        'norm_mem': gain((DEPTH, d)),
        'ab_w_in': nrm((N_AB, d, 3 * dab + N_HEADS_B), d ** -0.5),
        'ab_forget_bias': FORGET_BIAS_INIT + 0.1 * nrm((N_AB, N_HEADS_B)),
        'ab_qk_gain': gain((N_AB, 4, HEAD_DIM)),
        'ab_rel_bias': nrm((N_AB, 2 * REL_CLIP + 1, N_HEADS_A), 0.5),
        'ab_w_out': nrm((N_AB, dab, d), dab ** -0.5),
        'sb_w_in': nrm((N_SB, d, 3 * dc), d ** -0.5),
        'sb_w_out': nrm((N_SB, dc, d), dc ** -0.5),
        'x_w_q': nrm((DEPTH, d, dx), d ** -0.5),
        'x_w_kv': nrm((DEPTH, d, 2 * dx), d ** -0.5),
        'x_qk_gain': gain((DEPTH, 2, HEAD_DIM_X)),
        'x_w_o': nrm((DEPTH, dx, d), dx ** -0.5),
        'mlp_w_up': nrm((DEPTH, d, D_FF), d ** -0.5),
        'mlp_w_down': nrm((DEPTH, D_FF, d), D_FF ** -0.5),
    }


def reference(x_prompt, x_sample, mem_prompt, cache_a_k, cache_a_v, cache_b_k, cache_b_v, cache_b_logf,
              cache_c_k, cache_c_v, cache_mem_k, cache_mem_v, norm_mix, norm_cross, norm_mlp, norm_mem,
              ab_w_in, ab_forget_bias, ab_qk_gain, ab_rel_bias, ab_w_out, sb_w_in, sb_w_out,
              x_w_q, x_w_kv, x_qk_gain, x_w_o, mlp_w_up, mlp_w_down):
    hp, hs = x_prompt, x_sample
    a_kp, a_vp, a_ks, a_vs = [], [], [], []
    b_kp, b_vp, b_fp, b_ks, b_vs, b_fs = [], [], [], [], [], []
    c_kp, c_vp, c_ks, c_vs = [], [], [], []
    m_kp, m_vp = [], []
    for layer in range(DEPTH):
        g = norm_mix[layer]
        i = layer // 2
        if layer % 2 == 0:
            dp, (ak, av, bk, bv, bf) = ab_prompt(rmsnorm(hp, g), ab_w_in[i], ab_forget_bias[i],
                                                 ab_qk_gain[i], ab_rel_bias[i], ab_w_out[i])
            ds, (ak2, av2, bk2, bv2, bf2) = ab_sample(rmsnorm(hs, g), cache_a_k[i], cache_a_v[i],
                                                      cache_b_k[i], cache_b_v[i], cache_b_logf[i],
                                                      ab_w_in[i], ab_forget_bias[i], ab_qk_gain[i],
                                                      ab_rel_bias[i], ab_w_out[i])
            a_kp.append(ak); a_vp.append(av); a_ks.append(ak2); a_vs.append(av2)
            b_kp.append(bk); b_vp.append(bv); b_fp.append(bf)
            b_ks.append(bk2); b_vs.append(bv2); b_fs.append(bf2)
        else:
            dp, (ck, cv) = sb_prompt(rmsnorm(hp, g), sb_w_in[i], sb_w_out[i])
            ds, (ck2, cv2) = sb_sample(rmsnorm(hs, g), cache_c_k[i], cache_c_v[i], sb_w_in[i], sb_w_out[i])
            c_kp.append(ck); c_vp.append(cv); c_ks.append(ck2); c_vs.append(cv2)
        hp = hp + dp
        hs = hs + ds
        mk, mv = mem_kv(mem_prompt, norm_mem[layer], x_w_kv[layer], x_qk_gain[layer, 1])
        m_kp.append(mk); m_vp.append(mv)
        hp = hp + cross_attend(rmsnorm(hp, norm_cross[layer]), mk, mv, x_w_q[layer], x_qk_gain[layer, 0], x_w_o[layer])
        hs = hs + cross_attend(rmsnorm(hs, norm_cross[layer]), cache_mem_k[layer], cache_mem_v[layer],
                               x_w_q[layer], x_qk_gain[layer, 0], x_w_o[layer])
        hp = hp + mlp(rmsnorm(hp, norm_mlp[layer]), mlp_w_up[layer], mlp_w_down[layer])
        hs = hs + mlp(rmsnorm(hs, norm_mlp[layer]), mlp_w_up[layer], mlp_w_down[layer])
    return (hp, hs,
            jnp.stack(a_kp), jnp.stack(a_vp), jnp.stack(a_ks), jnp.stack(a_vs),
            jnp.stack(b_kp), jnp.stack(b_vp), jnp.stack(b_fp),
            jnp.stack(b_ks), jnp.stack(b_vs), jnp.stack(b_fs),
            jnp.stack(c_kp), jnp.stack(c_vp), jnp.stack(c_ks), jnp.stack(c_vs),
            jnp.stack(m_kp), jnp.stack(m_vp))
```

```python
import functools

import numpy as np
import jax
import jax.numpy as jnp
from jax import lax
from jax.experimental import pallas as pl
from jax.experimental.pallas import tpu as pltpu

F32 = jnp.float32
BF16 = jnp.bfloat16

D_MODEL = 1024
HEAD_DIM = 64
LANES = 128
CHUNK = 64
A_LEFT_CHUNKS = 8
A_PAST = A_LEFT_CHUNKS * CHUNK
REL_CLIP = 128
N_HEADS_X = 4
HEAD_DIM_X = D_MODEL // N_HEADS_X
D_FF = 4 * D_MODEL
EPS = 1e-6
NEG = -1e30
QK_SCALE = HEAD_DIM ** -0.5
X_SCALE = HEAD_DIM_X ** -0.5

A_TQ = 128
A_WIN = A_PAST + A_TQ
ATT_T = 256
FF_CHUNK = 1024
VMEM_LIMIT = 56 * 1024 * 1024


def _row_tile(n, cap=512):
    t = cap
    while n % t:
        t //= 2
    return t


def _params(n_axes, vmem=None):
    return pltpu.CompilerParams(dimension_semantics=("arbitrary",) * n_axes,
                                vmem_limit_bytes=vmem)


def _const_spec(shape):
    nd = len(shape)
    return pl.BlockSpec(shape, lambda *_: (0,) * nd, pipeline_mode=pl.Buffered(1))


def _rms_rows(x, g):
    ms = jnp.mean(x * x, axis=-1, keepdims=True)
    return x * lax.rsqrt(ms + EPS) * g


def _log_sigmoid(x):
    return jnp.minimum(x, 0.0) - jnp.log1p(jnp.exp(-jnp.abs(x)))


def _lo_mask():
    return lax.broadcasted_iota(jnp.int32, (1, LANES), 1) < HEAD_DIM


def _pair_headnorm(blk, gain, lo):
    sq = blk * blk
    s_lo = jnp.sum(jnp.where(lo, sq, 0.0), axis=-1, keepdims=True)
    s_hi = jnp.sum(jnp.where(lo, 0.0, sq), axis=-1, keepdims=True)
    inv = jnp.where(lo, lax.rsqrt(s_lo * (1.0 / HEAD_DIM) + EPS),
                    lax.rsqrt(s_hi * (1.0 / HEAD_DIM) + EPS))
    return blk * inv * gain


def _half(x, keep_lanes):
    return jnp.where(keep_lanes, x.astype(F32), 0.0).astype(BF16)


def _dot_nt(a, b):
    return lax.dot_general(a, b, (((1,), (1,)), ((), ())), preferred_element_type=F32)


def _proj_ab_kernel(h_ref, g_ref, w_ref, wf_ref, bf_ref, gain_ref,
                    qa_ref, ka_ref, va_ref, qb_ref, kb_ref, vb_ref, lf_ref):
    xn = _rms_rows(h_ref[...], g_ref[...]).astype(BF16)
    lo = _lo_mask()
    seg_w = w_ref.shape[1] // 6
    outs = ((qa_ref, 0, True), (ka_ref, 1, False), (va_ref, None, False),
            (qb_ref, 2, True), (kb_ref, 3, False), (vb_ref, None, False))
    for s, (ref, gi, is_q) in enumerate(outs):
        y = jnp.dot(xn, w_ref[:, s * seg_w:(s + 1) * seg_w], preferred_element_type=F32)
        if gi is None:
            ref[...] = y
            continue
        gain = gain_ref[gi:gi + 1, :]
        for p in range(seg_w // LANES):
            blk = _pair_headnorm(y[:, p * LANES:(p + 1) * LANES], gain, lo)
            if is_q:
                blk = blk * QK_SCALE
            ref[:, p * LANES:(p + 1) * LANES] = blk.astype(ref.dtype)
    fl = jnp.dot(xn, wf_ref[...], preferred_element_type=F32) + bf_ref[...]
    lf_ref[...] = _log_sigmoid(fl)


def _proj_ab(h2d, g, w_main, wf, bf, gains):
    n = h2d.shape[0]
    ts = _row_tile(n)
    dab = w_main.shape[1] // 6
    row = lambda w: pl.BlockSpec((ts, w), lambda i: (i, 0))
    out_shape = (jax.ShapeDtypeStruct((n, dab), BF16), jax.ShapeDtypeStruct((n, dab), F32),
                 jax.ShapeDtypeStruct((n, dab), F32), jax.ShapeDtypeStruct((n, dab), BF16),
                 jax.ShapeDtypeStruct((n, dab), F32), jax.ShapeDtypeStruct((n, dab), F32),
                 jax.ShapeDtypeStruct((n, LANES), F32))
    return pl.pallas_call(
        _proj_ab_kernel,
        grid=(n // ts,),
        in_specs=[row(D_MODEL), _const_spec(g.shape), _const_spec(w_main.shape),
                  _const_spec(wf.shape), _const_spec(bf.shape), _const_spec(gains.shape)],
        out_specs=[row(dab)] * 6 + [row(LANES)],
        out_shape=out_shape,
        compiler_params=_params(1, VMEM_LIMIT),
        name="proj_ab",
    )(h2d, g, w_main, wf, bf, gains)


def _proj_sb_kernel(h_ref, g_ref, w_ref, q_ref, k_ref, v_ref):
    xn = _rms_rows(h_ref[...], g_ref[...]).astype(BF16)
    dc = w_ref.shape[1] // 3
    q = jnp.dot(xn, w_ref[:, :dc], preferred_element_type=F32)
    q_ref[...] = (q * QK_SCALE).astype(BF16)
    k_ref[...] = jnp.dot(xn, w_ref[:, dc:2 * dc], preferred_element_type=F32)
    v_ref[...] = jnp.dot(xn, w_ref[:, 2 * dc:], preferred_element_type=F32)


def _proj_sb(h2d, g, w):
    n = h2d.shape[0]
    ts = _row_tile(n)
    dc = w.shape[1] // 3
    row = lambda width: pl.BlockSpec((ts, width), lambda i: (i, 0))
    return pl.pallas_call(
        _proj_sb_kernel,
        grid=(n // ts,),
        in_specs=[row(D_MODEL), _const_spec(g.shape), _const_spec(w.shape)],
        out_specs=[row(dc)] * 3,
        out_shape=(jax.ShapeDtypeStruct((n, dc), BF16), jax.ShapeDtypeStruct((n, dc), F32),
                   jax.ShapeDtypeStruct((n, dc), F32)),
        compiler_params=_params(1, VMEM_LIMIT),
        name="proj_sb",
    )(h2d, g, w)


def _split3(x):
    hi = x.astype(BF16)
    r = x - hi.astype(F32)
    mid = r.astype(BF16)
    lo = (r - mid.astype(F32)).astype(BF16)
    return hi, mid, lo


def _cumsum_kernel(x_ref, o_ref):
    rows, length = x_ref.shape
    r_i = lax.broadcasted_iota(jnp.int32, (LANES, LANES), 0)
    c_i = lax.broadcasted_iota(jnp.int32, (LANES, LANES), 1)
    tri = jnp.where(r_i <= c_i, 1.0, 0.0).astype(BF16)
    carry = jnp.zeros((rows, 1), F32)
    for b in range(length // LANES):
        hi, mid, lo = _split3(x_ref[:, b * LANES:(b + 1) * LANES])
        c = (jnp.dot(hi, tri, preferred_element_type=F32)
             + jnp.dot(mid, tri, preferred_element_type=F32)
             + jnp.dot(lo, tri, preferred_element_type=F32)) + carry
        o_ref[:, b * LANES:(b + 1) * LANES] = c
        carry = c[:, LANES - 1:LANES]


def _cumsum_rows(x):
    return pl.pallas_call(
        _cumsum_kernel,
        out_shape=jax.ShapeDtypeStruct(x.shape, F32),
        name="cumsum_logf",
    )(x)


def _softmax_rows(x):
    m = jnp.max(x, axis=-1, keepdims=True)
    p = jnp.exp(x - m)
    return p, jnp.sum(p, axis=-1, keepdims=True)


def _attn_a_prompt_kernel(q_ref, k_ref, v_ref, bias_ref, o_ref, kb, vb, *, nq, win, ncase):
    lo = _lo_mask()
    kb[...] = k_ref[0].astype(BF16)
    v = v_ref[0]
    vb[0] = jnp.where(lo, v, 0.0).astype(BF16)
    vb[1] = jnp.where(lo, 0.0, v).astype(BF16)

    def q_tile(i, carry):
        q0 = pl.multiple_of(i * A_TQ, A_TQ)
        w0 = pl.multiple_of(jnp.maximum(q0 + A_TQ - win, 0), A_TQ)
        case = jnp.minimum(i, ncase - 1)
        q = q_ref[0, pl.ds(q0, A_TQ), :]
        k = kb[pl.ds(w0, win), :]
        out = jnp.zeros((A_TQ, LANES), F32)
        for e in range(2):
            qe = _half(q, lo if e == 0 else jnp.logical_not(lo))
            x = _dot_nt(qe, k) + bias_ref[0, case, e]
            p, l = _softmax_rows(x)
            o = jnp.dot(p.astype(BF16), vb[e, pl.ds(w0, win), :], preferred_element_type=F32)
            out = out + o / l
        o_ref[0, pl.ds(q0, A_TQ), :] = out.astype(o_ref.dtype)
        return carry

    lax.fori_loop(0, nq, q_tile, 0)


def _band_bias_prompt(rel_bias, s_len):
    win = min(A_WIN, s_len)
    nq = s_len // A_TQ
    ncase = min(nq, A_PAST // A_TQ + 1)
    r = np.arange(A_TQ)[None, :, None]
    j = np.arange(win)[None, None, :]
    c = np.arange(ncase)[:, None, None]
    q_rel = np.where(c * A_TQ + A_TQ <= win, c * A_TQ, win - A_TQ) + r
    qc, kc = q_rel // CHUNK, j // CHUNK
    ok = (kc <= qc) & (kc >= qc - A_LEFT_CHUNKS)
    idx = np.clip(j - q_rel, -REL_CLIP, REL_CLIP) + REL_CLIP
    tab = jnp.where(ok[..., None], rel_bias[idx], NEG)
    tab = jnp.transpose(tab, (3, 0, 1, 2))
    h = tab.shape[0]
    tab = tab.reshape(h // 2, 2, ncase, A_TQ, win)
    return jnp.transpose(tab, (0, 2, 1, 3, 4)).astype(F32), win, ncase


def _attn_a_prompt(q, k, v, rel_bias):
    b, s, da = q.shape
    pairs = da // LANES
    bias, win, ncase = _band_bias_prompt(rel_bias, s)
    nq = s // A_TQ
    blk = lambda: pl.BlockSpec((1, s, LANES), lambda bi, p: (bi, 0, p))
    return pl.pallas_call(
        functools.partial(_attn_a_prompt_kernel, nq=nq, win=win, ncase=ncase),
        grid=(b, pairs),
        in_specs=[blk(), blk(), blk(),
                  pl.BlockSpec((1, ncase, 2, A_TQ, win), lambda bi, p: (p, 0, 0, 0, 0))],
        out_specs=blk(),
        out_shape=jax.ShapeDtypeStruct((b, s, da), BF16),
        scratch_shapes=[pltpu.VMEM((s, LANES), BF16), pltpu.VMEM((2, s, LANES), BF16)],
        compiler_params=_params(2, VMEM_LIMIT),
        name="attn_a_prompt",
    )(q, k, v, bias)


def _attn_a_sample_kernel(q_ref, kc_ref, vc_ref, kn_ref, vn_ref, bc_ref, bn_ref, o_ref):
    lo = _lo_mask()
    q = q_ref[0]
    kc = kc_ref[0].astype(BF16)
    kn = kn_ref[0].astype(BF16)
    vc, vn = vc_ref[0], vn_ref[0]
    out = jnp.zeros(q.shape, F32)
    for e in range(2):
        he = lo if e == 0 else jnp.logical_not(lo)
        qe = _half(q, he)
        xc = _dot_nt(qe, kc) + bc_ref[0, e]
        xn = _dot_nt(qe, kn) + bn_ref[0, e]
        m = jnp.maximum(jnp.max(xc, axis=-1, keepdims=True), jnp.max(xn, axis=-1, keepdims=True))
        pc, pn = jnp.exp(xc - m), jnp.exp(xn - m)
        l = jnp.sum(pc, axis=-1, keepdims=True) + jnp.sum(pn, axis=-1, keepdims=True)
        o = (jnp.dot(pc.astype(BF16), jnp.where(he, vc, 0.0).astype(BF16), preferred_element_type=F32)
             + jnp.dot(pn.astype(BF16), jnp.where(he, vn, 0.0).astype(BF16), preferred_element_type=F32))
        out = out + o / l
    o_ref[0] = out.astype(o_ref.dtype)


def _attn_a_sample(q, k_new, v_new, k_cache, v_cache, rel_bias, past):
    b, t, da = q.shape
    pairs = da // LANES
    n_band = k_cache.shape[1]
    q_pos = past + np.arange(t)[:, None]
    k_pos = past - n_band + np.arange(n_band + t)[None, :]
    qc, kc = q_pos // CHUNK, k_pos // CHUNK
    ok = (k_pos >= 0) & (kc <= qc) & (kc >= qc - A_LEFT_CHUNKS)
    idx = np.clip(k_pos - q_pos, -REL_CLIP, REL_CLIP) + REL_CLIP
    tab = jnp.where(ok[..., None], rel_bias[idx], NEG)
    tab = jnp.transpose(tab, (2, 0, 1)).reshape(pairs, 2, t, n_band + t).astype(F32)
    bias_c, bias_n = tab[..., :n_band], tab[..., n_band:]
    new = lambda: pl.BlockSpec((1, t, LANES), lambda bi, p: (bi, 0, p))
    old = lambda: pl.BlockSpec((1, n_band, LANES), lambda bi, p: (bi, 0, p))
    return pl.pallas_call(
        _attn_a_sample_kernel,
        grid=(b, pairs),
        in_specs=[new(), old(), old(), new(), new(),
                  pl.BlockSpec((1, 2, t, n_band), lambda bi, p: (p, 0, 0, 0)),
                  pl.BlockSpec((1, 2, t, t), lambda bi, p: (p, 0, 0, 0))],
        out_specs=new(),
        out_shape=jax.ShapeDtypeStruct((b, t, da), BF16),
        compiler_params=_params(2, VMEM_LIMIT),
        name="attn_a_sample",
    )(q, k_cache, v_cache, k_new, v_new, bias_c, bias_n)


def _causal_mask(t):
    r = lax.broadcasted_iota(jnp.int32, (t, t), 0)
    c = lax.broadcasted_iota(jnp.int32, (t, t), 1)
    return c <= r, c < r


def _attn_b_kernel(*refs, tq, nq, npre):
    if npre:
        (q_ref, k_ref, v_ref, crow_ref, ccol_ref, kp_ref, vp_ref, cpre_ref, o_ref,
         kb, vb, m_s, l_s, acc_s, kpb, vpb) = refs
    else:
        q_ref, k_ref, v_ref, crow_ref, ccol_ref, o_ref, kb, vb, m_s, l_s, acc_s = refs
    pair = pl.program_id(1)
    lo = _lo_mask()
    kb[...] = k_ref[0].astype(BF16)
    v = v_ref[0]
    vb[0] = jnp.where(lo, v, 0.0).astype(BF16)
    vb[1] = jnp.where(lo, 0.0, v).astype(BF16)
    if npre:
        kpb[...] = kp_ref[0].astype(BF16)
        vp = vp_ref[0]
        vpb[0] = jnp.where(lo, vp, 0.0).astype(BF16)
        vpb[1] = jnp.where(lo, 0.0, vp).astype(BF16)
    tpre = kp_ref.shape[1] // npre if npre else 0
    keep, _ = _causal_mask(tq)
    head_lane = lax.broadcasted_iota(jnp.int32, (1, ccol_ref.shape[2]), 1)

    def q_tile(i, carry):
        q0 = pl.multiple_of(i * tq, tq)
        q = q_ref[0, pl.ds(q0, tq), :]
        cq_all = ccol_ref[0, pl.ds(q0, tq), :]
        out = jnp.zeros((tq, LANES), F32)
        for e in range(2):
            qe = _half(q, lo if e == 0 else jnp.logical_not(lo))
            cq = jnp.sum(jnp.where(head_lane == 2 * pair + e, cq_all, 0.0), axis=-1, keepdims=True)
            m_s[...] = jnp.full(m_s.shape, NEG, F32)
            l_s[...] = jnp.zeros(l_s.shape, F32)
            acc_s[...] = jnp.zeros(acc_s.shape, F32)

            def step(k, vv, ck, masked):
                x = _dot_nt(qe, k) + (cq - ck)
                if masked:
                    x = jnp.where(keep, x, NEG)
                m_old = m_s[...]
                m_new = jnp.maximum(m_old, jnp.max(x, axis=-1, keepdims=True))
                alpha = jnp.exp(m_old - m_new)
                p = jnp.exp(x - m_new)
                l_s[...] = alpha * l_s[...] + jnp.sum(p, axis=-1, keepdims=True)
                acc_s[...] = alpha * acc_s[...] + jnp.dot(p.astype(BF16), vv, preferred_element_type=F32)
                m_s[...] = m_new

            if npre:
                def pre_step(j, c):
                    k0 = pl.multiple_of(j * tpre, tpre)
                    step(kpb[pl.ds(k0, tpre), :], vpb[e, pl.ds(k0, tpre), :],
                         cpre_ref[0, 0, e, pl.ds(j, 1), :], False)
                    return c
                lax.fori_loop(0, npre, pre_step, 0)

            def self_step(j, c):
                k0 = pl.multiple_of(j * tq, tq)
                step(kb[pl.ds(k0, tq), :], vb[e, pl.ds(k0, tq), :],
                     crow_ref[0, 0, e, pl.ds(j, 1), :], False)
                return c
            lax.fori_loop(0, i, self_step, 0)
            step(kb[pl.ds(q0, tq), :], vb[e, pl.ds(q0, tq), :],
                 crow_ref[0, 0, e, pl.ds(i, 1), :], True)
            out = out + acc_s[...] / l_s[...]
        o_ref[0, pl.ds(q0, tq), :] = out.astype(o_ref.dtype)
        return carry

    lax.fori_loop(0, nq, q_tile, 0)


def _attn_b(q, k, v, c_self, k_pre=None, v_pre=None, c_pre=None):
    b, t, db = q.shape
    pairs = db // LANES
    n_heads = c_self.shape[1]
    tq = min(ATT_T, t)
    nq = t // tq
    c_col = jnp.transpose(c_self, (0, 2, 1))
    c_row = c_self.reshape(b, pairs, 2, nq, tq)
    new = lambda: pl.BlockSpec((1, t, LANES), lambda bi, p: (bi, 0, p))
    in_specs = [new(), new(), new(),
                pl.BlockSpec((1, 1, 2, nq, tq), lambda bi, p: (bi, p, 0, 0, 0)),
                pl.BlockSpec((1, t, n_heads), lambda bi, p: (bi, 0, 0))]
    args = [q, k, v, c_row, c_col]
    scratch = [pltpu.VMEM((t, LANES), BF16), pltpu.VMEM((2, t, LANES), BF16),
               pltpu.VMEM((tq, 1), F32), pltpu.VMEM((tq, 1), F32), pltpu.VMEM((tq, LANES), F32)]
    npre = 0
    if k_pre is not None:
        plen = k_pre.shape[1]
        tpre = min(ATT_T, plen)
        npre = plen // tpre
        old = lambda: pl.BlockSpec((1, plen, LANES), lambda bi, p: (bi, 0, p))
        in_specs += [old(), old(), pl.BlockSpec((1, 1, 2, npre, tpre), lambda bi, p: (bi, p, 0, 0, 0))]
        args += [k_pre, v_pre, c_pre.reshape(b, pairs, 2, npre, tpre)]
        scratch += [pltpu.VMEM((plen, LANES), BF16), pltpu.VMEM((2, plen, LANES), BF16)]
    return pl.pallas_call(
        functools.partial(_attn_b_kernel, tq=tq, nq=nq, npre=npre),
        grid=(b, pairs),
        in_specs=in_specs,
        out_specs=new(),
        out_shape=jax.ShapeDtypeStruct((b, t, db), BF16),
        scratch_shapes=scratch,
        compiler_params=_params(2, VMEM_LIMIT),
        name="attn_b",
    )(*args)


def _attn_c_kernel(*refs, tq, nq, npre):
    if npre:
        q_ref, k_ref, v_ref, kp_ref, vp_ref, o_ref, kb, vb, r_s, acc_s, u_self, kpb, vpb, u_pre = refs
    else:
        q_ref, k_ref, v_ref, o_ref, kb, vb, r_s, acc_s, u_self = refs
    lo = _lo_mask()

    def later_ones(t):
        r = lax.broadcasted_iota(jnp.int32, (t, t), 0)
        c = lax.broadcasted_iota(jnp.int32, (t, t), 1)
        return jnp.where(r > c, 1.0, 0.0).astype(BF16)

    u_self[...] = later_ones(tq)
    if npre:
        u_pre[...] = later_ones(u_pre.shape[0])
    kb[...] = k_ref[0].astype(BF16)
    v = v_ref[0]
    vb[0] = jnp.where(lo, v, 0.0).astype(BF16)
    vb[1] = jnp.where(lo, 0.0, v).astype(BF16)
    if npre:
        kpb[...] = kp_ref[0].astype(BF16)
        vp = vp_ref[0]
        vpb[0] = jnp.where(lo, vp, 0.0).astype(BF16)
        vpb[1] = jnp.where(lo, 0.0, vp).astype(BF16)
    tpre = kp_ref.shape[1] // npre if npre else 0
    _, before = _causal_mask(tq)

    def q_tile(i, carry):
        q0 = pl.multiple_of(i * tq, tq)
        q = q_ref[0, pl.ds(q0, tq), :]
        out = jnp.zeros((tq, LANES), F32)
        for e in range(2):
            qe = _half(q, lo if e == 0 else jnp.logical_not(lo))
            r_s[...] = jnp.zeros(r_s.shape, F32)
            acc_s[...] = jnp.zeros(acc_s.shape, F32)

            def step(k, vv, ones, masked):
                z = _dot_nt(qe, k)
                log_keep = -(jnp.maximum(z, 0.0) + jnp.log1p(jnp.exp(-jnp.abs(z))))
                log_beta = log_keep + z
                if masked:
                    log_keep = jnp.where(before, log_keep, 0.0)
                hi, mid, lo3 = _split3(log_keep)
                inner = (jnp.dot(hi, ones, preferred_element_type=F32)
                         + jnp.dot(mid, ones, preferred_element_type=F32)
                         + jnp.dot(lo3, ones, preferred_element_type=F32))
                w = jnp.exp(log_beta + (inner + r_s[...]))
                if masked:
                    w = jnp.where(before, w, 0.0)
                acc_s[...] = acc_s[...] + jnp.dot(w.astype(BF16), vv, preferred_element_type=F32)
                r_s[...] = r_s[...] + (inner[:, 0:1] + log_keep[:, 0:1])

            step(kb[pl.ds(q0, tq), :], vb[e, pl.ds(q0, tq), :], u_self[...], True)

            def self_step(jj, c):
                k0 = pl.multiple_of((i - 1 - jj) * tq, tq)
                step(kb[pl.ds(k0, tq), :], vb[e, pl.ds(k0, tq), :], u_self[...], False)
                return c
            lax.fori_loop(0, i, self_step, 0)

            if npre:
                def pre_step(jj, c):
                    k0 = pl.multiple_of((npre - 1 - jj) * tpre, tpre)
                    step(kpb[pl.ds(k0, tpre), :], vpb[e, pl.ds(k0, tpre), :], u_pre[...], False)
                    return c
                lax.fori_loop(0, npre, pre_step, 0)
            out = out + acc_s[...]
        o_ref[0, pl.ds(q0, tq), :] = out.astype(o_ref.dtype)
        return carry

    lax.fori_loop(0, nq, q_tile, 0)


def _attn_c(q, k, v, k_pre=None, v_pre=None):
    b, t, dc = q.shape
    pairs = dc // LANES
    tq = min(ATT_T, t)
    nq = t // tq
    new = lambda: pl.BlockSpec((1, t, LANES), lambda bi, p: (bi, 0, p))
    in_specs = [new(), new(), new()]
    args = [q, k, v]
    scratch = [pltpu.VMEM((t, LANES), BF16), pltpu.VMEM((2, t, LANES), BF16),
               pltpu.VMEM((tq, 1), F32), pltpu.VMEM((tq, LANES), F32), pltpu.VMEM((tq, tq), BF16)]
    npre = 0
    if k_pre is not None:
        plen = k_pre.shape[1]
        tpre = min(ATT_T, plen)
        npre = plen // tpre
        old = lambda: pl.BlockSpec((1, plen, LANES), lambda bi, p: (bi, 0, p))
        in_specs += [old(), old()]
        args += [k_pre, v_pre]
        scratch += [pltpu.VMEM((plen, LANES), BF16), pltpu.VMEM((2, plen, LANES), BF16),
                    pltpu.VMEM((tpre, tpre), BF16)]
    return pl.pallas_call(
        functools.partial(_attn_c_kernel, tq=tq, nq=nq, npre=npre),
        grid=(b, pairs),
        in_specs=in_specs,
        out_specs=new(),
        out_shape=jax.ShapeDtypeStruct((b, t, dc), BF16),
        scratch_shapes=scratch,
        compiler_params=_params(2, VMEM_LIMIT),
        name="attn_c",
    )(*args)


def _head_norm_x(y, gain):
    parts = []
    for h in range(N_HEADS_X):
        blk = y[:, h * HEAD_DIM_X:(h + 1) * HEAD_DIM_X]
        ms = jnp.mean(blk * blk, axis=-1, keepdims=True)
        parts.append(blk * lax.rsqrt(ms + EPS) * gain)
    return parts


def _mem_kv_kernel(m_ref, g_ref, w_ref, gain_ref, k_ref, v_ref, kb_ref, vb_ref):
    xn = _rms_rows(m_ref[...], g_ref[...]).astype(BF16)
    dx = w_ref.shape[1] // 2
    k = jnp.dot(xn, w_ref[:, :dx], preferred_element_type=F32)
    for h, blk in enumerate(_head_norm_x(k, gain_ref[...])):
        k_ref[:, h * HEAD_DIM_X:(h + 1) * HEAD_DIM_X] = blk
        kb_ref[:, h * HEAD_DIM_X:(h + 1) * HEAD_DIM_X] = blk.astype(BF16)
    v = jnp.dot(xn, w_ref[:, dx:], preferred_element_type=F32)
    v_ref[...] = v
    vb_ref[...] = v.astype(BF16)


def _mem_kv(mem2d, g, w_kv, k_gain):
    n = mem2d.shape[0]
    ts = _row_tile(n)
    dx = w_kv.shape[1] // 2
    row = lambda width: pl.BlockSpec((ts, width), lambda i: (i, 0))
    return pl.pallas_call(
        _mem_kv_kernel,
        grid=(n // ts,),
        in_specs=[row(D_MODEL), _const_spec(g.shape), _const_spec(w_kv.shape), _const_spec(k_gain.shape)],
        out_specs=[row(dx)] * 4,
        out_shape=(jax.ShapeDtypeStruct((n, dx), F32), jax.ShapeDtypeStruct((n, dx), F32),
                   jax.ShapeDtypeStruct((n, dx), BF16), jax.ShapeDtypeStruct((n, dx), BF16)),
        compiler_params=_params(1, VMEM_LIMIT),
        name="mem_kv",
    )(mem2d, g, w_kv, k_gain)


def _post_kernel(h_ref, o1_ref, o2_ref, wout_ref, g_ref, wq_ref, qgain_ref, mk_ref, mv_ref, wo_ref, y_ref):
    half = o1_ref.shape[2]
    h1 = (h_ref[0]
          + jnp.dot(o1_ref[0], wout_ref[:half, :], preferred_element_type=F32)
          + jnp.dot(o2_ref[0], wout_ref[half:, :], preferred_element_type=F32))
    hn = _rms_rows(h1, g_ref[...]).astype(BF16)
    qx = jnp.dot(hn, wq_ref[...], preferred_element_type=F32)
    heads = []
    for h, qh in enumerate(_head_norm_x(qx, qgain_ref[...])):
        cols = slice(h * HEAD_DIM_X, (h + 1) * HEAD_DIM_X)
        s = _dot_nt((qh * X_SCALE).astype(BF16), mk_ref[0, :, cols])
        p, l = _softmax_rows(s)
        oh = jnp.dot(p.astype(BF16), mv_ref[0, :, cols], preferred_element_type=F32) / l
        heads.append(oh.astype(BF16))
    y_ref[0] = h1 + jnp.dot(jnp.concatenate(heads, axis=-1), wo_ref[...], preferred_element_type=F32)


def _post(h, o1, o2, w_out, g_cross, w_q, q_gain, mk, mv, w_o):
    b, s, d = h.shape
    ts = _row_tile(s)
    half = w_out.shape[0] // 2
    n_mem = mk.shape[1]
    tile = lambda width: pl.BlockSpec((1, ts, width), lambda bi, i: (bi, i, 0))
    o1_spec = pl.BlockSpec((1, ts, half), lambda bi, i: (bi, i, 0))
    if o2 is None:
        o2 = o1
        o2_spec = pl.BlockSpec((1, ts, half), lambda bi, i: (bi, i, 1))
    else:
        o2_spec = o1_spec
    mem = lambda: pl.BlockSpec((1, n_mem, d), lambda bi, i: (bi, 0, 0))
    return pl.pallas_call(
        _post_kernel,
        grid=(b, s // ts),
        in_specs=[tile(d), o1_spec, o2_spec, _const_spec(w_out.shape), _const_spec(g_cross.shape),
                  _const_spec(w_q.shape), _const_spec(q_gain.shape), mem(), mem(), _const_spec(w_o.shape)],
        out_specs=tile(d),
        out_shape=jax.ShapeDtypeStruct((b, s, d), F32),
        compiler_params=_params(2, VMEM_LIMIT),
        name="post_mixer_cross",
    )(h, o1, o2, w_out, g_cross, w_q, q_gain, mk, mv, w_o)


def _mlp_kernel(h_ref, g_ref, wup_ref, wdn_ref, y_ref):
    h = h_ref[...]
    hn = _rms_rows(h, g_ref[...]).astype(BF16)
    acc = h
    for c in range(wup_ref.shape[1] // FF_CHUNK):
        cols = slice(c * FF_CHUNK, (c + 1) * FF_CHUNK)
        u = jnp.maximum(jnp.dot(hn, wup_ref[:, cols], preferred_element_type=F32), 0.0)
        acc = acc + jnp.dot((u * u).astype(BF16), wdn_ref[cols, :], preferred_element_type=F32)
    y_ref[...] = acc


def _mlp(h2d, g, w_up, w_down):
    n = h2d.shape[0]
    ts = _row_tile(n)
    row = pl.BlockSpec((ts, D_MODEL), lambda i: (i, 0))
    return pl.pallas_call(
        _mlp_kernel,
        grid=(n // ts,),
        in_specs=[row, _const_spec(g.shape), _const_spec(w_up.shape), _const_spec(w_down.shape)],
        out_specs=row,
        out_shape=jax.ShapeDtypeStruct((n, D_MODEL), F32),
        compiler_params=_params(1, VMEM_LIMIT),
        name="mlp",
    )(h2d, g, w_up, w_down)


def _row_vec(x):
    return x.reshape(1, -1).astype(F32)


def _ab_weights(w_in, b_f, qk_gain):
    dab3 = w_in.shape[1] - b_f.shape[0]
    n_f = b_f.shape[0]
    w_main = w_in[:, :dab3].astype(BF16)
    wf = jnp.zeros((w_in.shape[0], LANES), F32).at[:, :n_f].set(w_in[:, dab3:]).astype(BF16)
    bf = jnp.zeros((1, LANES), F32).at[0, :n_f].set(b_f)
    gains = jnp.concatenate([qk_gain, qk_gain], axis=-1).astype(F32)
    return w_main, wf, bf, gains


def _heads_first(x):
    return jnp.transpose(x, (0, 2, 1))


def kernel(x_prompt, x_sample, mem_prompt, cache_a_k, cache_a_v, cache_b_k, cache_b_v, cache_b_logf,
           cache_c_k, cache_c_v, cache_mem_k, cache_mem_v, norm_mix, norm_cross, norm_mlp, norm_mem,
           ab_w_in, ab_forget_bias, ab_qk_gain, ab_rel_bias, ab_w_out, sb_w_in, sb_w_out,
           x_w_q, x_w_kv, x_qk_gain, x_w_o, mlp_w_up, mlp_w_down):
    bp, sp, d = x_prompt.shape
    bs, ss, _ = x_sample.shape
    depth = norm_mix.shape[0]
    n_mem = mem_prompt.shape[1]
    hb = ab_forget_bias.shape[1]
    hp, hs = x_prompt, x_sample
    outs = {name: [] for name in ("a_kp", "a_vp", "a_ks", "a_vs", "b_kp", "b_vp", "b_fp", "b_ks", "b_vs",
                                  "b_fs", "c_kp", "c_vp", "c_ks", "c_vs", "m_kp", "m_vp")}
    mem2d = mem_prompt.reshape(bp * n_mem, d)

    for layer in range(depth):
        i = layer // 2
        g_mix = _row_vec(norm_mix[layer])
        if layer % 2 == 0:
            w_main, wf, bf, gains = _ab_weights(ab_w_in[i], ab_forget_bias[i], ab_qk_gain[i])
            da = w_main.shape[1] // 6
            ha = da // HEAD_DIM
            qa, ka, va, qb, kb, vb, lf = _proj_ab(hp.reshape(bp * sp, d), g_mix, w_main, wf, bf, gains)
            r3 = lambda x, b_, s_: x.reshape(b_, s_, x.shape[-1])
            qa, ka, va, qb, kb, vb = (r3(x, bp, sp) for x in (qa, ka, va, qb, kb, vb))
            logf = r3(lf, bp, sp)[..., :hb]
            c_p = _cumsum_rows(_heads_first(logf).reshape(bp * hb, sp)).reshape(bp, hb, sp)
            oa = _attn_a_prompt(qa, ka, va, ab_rel_bias[i])
            ob = _attn_b(qb, kb, vb, c_p)
            keep = min(A_PAST, sp)
            outs["a_kp"].append(ka[:, sp - keep:].reshape(bp, keep, ha, HEAD_DIM))
            outs["a_vp"].append(va[:, sp - keep:].reshape(bp, keep, ha, HEAD_DIM))
            outs["b_kp"].append(kb.reshape(bp, sp, hb, HEAD_DIM))
            outs["b_vp"].append(vb.reshape(bp, sp, hb, HEAD_DIM))
            outs["b_fp"].append(logf)
            mix_p = (oa, ob)
            qa, ka, va, qb, kb, vb, lf = _proj_ab(hs.reshape(bs * ss, d), g_mix, w_main, wf, bf, gains)
            qa, ka, va, qb, kb, vb = (r3(x, bs, ss) for x in (qa, ka, va, qb, kb, vb))
            logf = r3(lf, bs, ss)[..., :hb]
            past = cache_b_k.shape[2]
            n_band = cache_a_k.shape[2]
            lf_all = jnp.concatenate([_heads_first(cache_b_logf[i].astype(F32)), _heads_first(logf)], axis=-1)
            total = past + ss
            padded = -(-total // LANES) * LANES
            lf_all = jnp.pad(lf_all, ((0, 0), (0, 0), (0, padded - total)))
            c_all = _cumsum_rows(lf_all.reshape(bs * hb, padded)).reshape(bs, hb, padded)
            oa = _attn_a_sample(qa, ka, va, cache_a_k[i].reshape(bs, n_band, da),
                                cache_a_v[i].reshape(bs, n_band, da), ab_rel_bias[i], past)
            ob = _attn_b(qb, kb, vb, c_all[..., past:total],
                         cache_b_k[i].reshape(bs, past, da), cache_b_v[i].reshape(bs, past, da),
                         c_all[..., :past])
            outs["a_ks"].append(ka.reshape(bs, ss, ha, HEAD_DIM))
            outs["a_vs"].append(va.reshape(bs, ss, ha, HEAD_DIM))
            outs["b_ks"].append(kb.reshape(bs, ss, hb, HEAD_DIM))
            outs["b_vs"].append(vb.reshape(bs, ss, hb, HEAD_DIM))
            outs["b_fs"].append(logf)
            mix_s = (oa, ob)
            w_out = ab_w_out[i].astype(BF16)
        else:
            w_sb = sb_w_in[i].astype(BF16)
            dc = w_sb.shape[1] // 3
            hc = dc // HEAD_DIM
            q, k, v = _proj_sb(hp.reshape(bp * sp, d), g_mix, w_sb)
            q, k, v = (x.reshape(bp, sp, dc) for x in (q, k, v))
            mix_p = (_attn_c(q, k, v), None)
            outs["c_kp"].append(k.reshape(bp, sp, hc, HEAD_DIM))
            outs["c_vp"].append(v.reshape(bp, sp, hc, HEAD_DIM))
            q, k, v = _proj_sb(hs.reshape(bs * ss, d), g_mix, w_sb)
            q, k, v = (x.reshape(bs, ss, dc) for x in (q, k, v))
            past = cache_c_k.shape[2]
            mix_s = (_attn_c(q, k, v, cache_c_k[i].reshape(bs, past, dc), cache_c_v[i].reshape(bs, past, dc)),
                     None)
            outs["c_ks"].append(k.reshape(bs, ss, hc, HEAD_DIM))
            outs["c_vs"].append(v.reshape(bs, ss, hc, HEAD_DIM))
            w_out = sb_w_out[i].astype(BF16)

        mk, mv, mkb, mvb = _mem_kv(mem2d, _row_vec(norm_mem[layer]), x_w_kv[layer].astype(BF16),
                                   _row_vec(x_qk_gain[layer, 1]))
        outs["m_kp"].append(mk.reshape(bp, n_mem, N_HEADS_X, HEAD_DIM_X))
        outs["m_vp"].append(mv.reshape(bp, n_mem, N_HEADS_X, HEAD_DIM_X))
        g_cross = _row_vec(norm_cross[layer])
        w_q, w_o = x_w_q[layer].astype(BF16), x_w_o[layer].astype(BF16)
        q_gain = _row_vec(x_qk_gain[layer, 0])
        hp = _post(hp, mix_p[0], mix_p[1], w_out, g_cross, w_q, q_gain,
                   mkb.reshape(bp, n_mem, d), mvb.reshape(bp, n_mem, d), w_o)
        hs = _post(hs, mix_s[0], mix_s[1], w_out, g_cross, w_q, q_gain,
                   cache_mem_k[layer].reshape(bs, n_mem, d).astype(BF16),
                   cache_mem_v[layer].reshape(bs, n_mem, d).astype(BF16), w_o)
        g_mlp = _row_vec(norm_mlp[layer])
        w_up, w_dn = mlp_w_up[layer].astype(BF16), mlp_w_down[layer].astype(BF16)
        hp = _mlp(hp.reshape(bp * sp, d), g_mlp, w_up, w_dn).reshape(bp, sp, d)
        hs = _mlp(hs.reshape(bs * ss, d), g_mlp, w_up, w_dn).reshape(bs, ss, d)

    st = lambda name: jnp.stack(outs[name])
    return (hp, hs, st("a_kp"), st("a_vp"), st("a_ks"), st("a_vs"),
            st("b_kp"), st("b_vp"), st("b_fp"), st("b_ks"), st("b_vs"), st("b_fs"),
            st("c_kp"), st("c_vp"), st("c_ks"), st("c_vs"), st("m_kp"), st("m_vp"))
```

```python
import functools

import numpy as np
import jax
import jax.numpy as jnp
from jax import lax
from jax.experimental import pallas as pl
from jax.experimental.pallas import tpu as pltpu

F32 = jnp.float32
BF16 = jnp.bfloat16

D_MODEL = 1024
HEAD_DIM = 64
LANES = 128
CHUNK = 64
A_LEFT_CHUNKS = 8
A_PAST = A_LEFT_CHUNKS * CHUNK
REL_CLIP = 128
N_HEADS_X = 4
HEAD_DIM_X = D_MODEL // N_HEADS_X
D_FF = 4 * D_MODEL
EPS = 1e-6
NEG = -1e30
QK_SCALE = HEAD_DIM ** -0.5
X_SCALE = HEAD_DIM_X ** -0.5

A_TQ = 128
A_WIN = A_PAST + A_TQ
ATT_T = 256
FF_CHUNK = 1024
VMEM_LIMIT = 56 * 1024 * 1024


def _row_tile(n, cap=512):
    t = cap
    while n % t:
        t //= 2
    return t


def _params(n_axes, vmem=None):
    return pltpu.CompilerParams(dimension_semantics=("arbitrary",) * n_axes,
                                vmem_limit_bytes=vmem)


def _const_spec(shape):
    nd = len(shape)
    return pl.BlockSpec(shape, lambda *_: (0,) * nd, pipeline_mode=pl.Buffered(1))


def _rms_rows(x, g):
    ms = jnp.mean(x * x, axis=-1, keepdims=True)
    return x * lax.rsqrt(ms + EPS) * g


def _log_sigmoid(x):
    return jnp.minimum(x, 0.0) - jnp.log1p(jnp.exp(-jnp.abs(x)))


def _lo_mask():
    return lax.broadcasted_iota(jnp.int32, (1, LANES), 1) < HEAD_DIM


def _pair_headnorm(blk, gain, lo):
    sq = blk * blk
    s_lo = jnp.sum(jnp.where(lo, sq, 0.0), axis=-1, keepdims=True)
    s_hi = jnp.sum(jnp.where(lo, 0.0, sq), axis=-1, keepdims=True)
    inv = jnp.where(lo, lax.rsqrt(s_lo * (1.0 / HEAD_DIM) + EPS),
                    lax.rsqrt(s_hi * (1.0 / HEAD_DIM) + EPS))
    return blk * inv * gain


def _dot_nt(a, b):
    return lax.dot_general(a, b, (((1,), (1,)), ((), ())), preferred_element_type=F32)


def _proj_ab_kernel(h_ref, g_ref, w_ref, wf_ref, bf_ref, gain_ref,
                    qa_ref, ka_ref, va_ref, qb_ref, kb_ref, vb_ref, lf_ref):
    xn = _rms_rows(h_ref[...], g_ref[...]).astype(BF16)
    lo = _lo_mask()
    seg_w = w_ref.shape[1] // 6
    outs = ((qa_ref, 0, True), (ka_ref, 1, False), (va_ref, None, False),
            (qb_ref, 2, True), (kb_ref, 3, False), (vb_ref, None, False))
    for s, (ref, gi, is_q) in enumerate(outs):
        y = jnp.dot(xn, w_ref[:, s * seg_w:(s + 1) * seg_w], preferred_element_type=F32)
        if gi is None:
            ref[...] = y
            continue
        gain = gain_ref[gi:gi + 1, :]
        for p in range(seg_w // LANES):
            blk = _pair_headnorm(y[:, p * LANES:(p + 1) * LANES], gain, lo)
            if is_q:
                blk = blk * QK_SCALE
            ref[:, p * LANES:(p + 1) * LANES] = blk.astype(ref.dtype)
    fl = jnp.dot(xn, wf_ref[...], preferred_element_type=F32) + bf_ref[...]
    lf_ref[...] = _log_sigmoid(fl)


def _proj_ab(h2d, g, w_main, wf, bf, gains):
    n = h2d.shape[0]
    ts = _row_tile(n)
    dab = w_main.shape[1] // 6
    row = lambda w: pl.BlockSpec((ts, w), lambda i: (i, 0))
    out_shape = (jax.ShapeDtypeStruct((n, dab), BF16), jax.ShapeDtypeStruct((n, dab), F32),
                 jax.ShapeDtypeStruct((n, dab), F32), jax.ShapeDtypeStruct((n, dab), BF16),
                 jax.ShapeDtypeStruct((n, dab), F32), jax.ShapeDtypeStruct((n, dab), F32),
                 jax.ShapeDtypeStruct((n, LANES), F32))
    return pl.pallas_call(
        _proj_ab_kernel,
        grid=(n // ts,),
        in_specs=[row(D_MODEL), _const_spec(g.shape), _const_spec(w_main.shape),
                  _const_spec(wf.shape), _const_spec(bf.shape), _const_spec(gains.shape)],
        out_specs=[row(dab)] * 6 + [row(LANES)],
        out_shape=out_shape,
        compiler_params=_params(1, VMEM_LIMIT),
        name="proj_ab",
    )(h2d, g, w_main, wf, bf, gains)


def _proj_sb_kernel(h_ref, g_ref, w_ref, q_ref, k_ref, v_ref):
    xn = _rms_rows(h_ref[...], g_ref[...]).astype(BF16)
    dc = w_ref.shape[1] // 3
    q = jnp.dot(xn, w_ref[:, :dc], preferred_element_type=F32)
    q_ref[...] = (q * QK_SCALE).astype(BF16)
    k_ref[...] = jnp.dot(xn, w_ref[:, dc:2 * dc], preferred_element_type=F32)
    v_ref[...] = jnp.dot(xn, w_ref[:, 2 * dc:], preferred_element_type=F32)


def _proj_sb(h2d, g, w):
    n = h2d.shape[0]
    ts = _row_tile(n)
    dc = w.shape[1] // 3
    row = lambda width: pl.BlockSpec((ts, width), lambda i: (i, 0))
    return pl.pallas_call(
        _proj_sb_kernel,
        grid=(n // ts,),
        in_specs=[row(D_MODEL), _const_spec(g.shape), _const_spec(w.shape)],
        out_specs=[row(dc)] * 3,
        out_shape=(jax.ShapeDtypeStruct((n, dc), BF16), jax.ShapeDtypeStruct((n, dc), F32),
                   jax.ShapeDtypeStruct((n, dc), F32)),
        compiler_params=_params(1, VMEM_LIMIT),
        name="proj_sb",
    )(h2d, g, w)


def _split3(x):
    hi = x.astype(BF16)
    r = x - hi.astype(F32)
    mid = r.astype(BF16)
    lo = (r - mid.astype(F32)).astype(BF16)
    return hi, mid, lo


def _cumsum_kernel(x_ref, o_ref):
    rows, length = x_ref.shape
    r_i = lax.broadcasted_iota(jnp.int32, (LANES, LANES), 0)
    c_i = lax.broadcasted_iota(jnp.int32, (LANES, LANES), 1)
    tri = jnp.where(r_i <= c_i, 1.0, 0.0).astype(BF16)
    carry = jnp.zeros((rows, 1), F32)
    for b in range(length // LANES):
        hi, mid, lo = _split3(x_ref[:, b * LANES:(b + 1) * LANES])
        c = (jnp.dot(hi, tri, preferred_element_type=F32)
             + jnp.dot(mid, tri, preferred_element_type=F32)
             + jnp.dot(lo, tri, preferred_element_type=F32)) + carry
        o_ref[:, b * LANES:(b + 1) * LANES] = c
        carry = c[:, LANES - 1:LANES]


def _cumsum_rows(x):
    return pl.pallas_call(
        _cumsum_kernel,
        out_shape=jax.ShapeDtypeStruct(x.shape, F32),
        name="cumsum_logf",
    )(x)


def _prep_kv(k_ref, v_ref, kt, vb):
    kt[...] = jnp.transpose(k_ref[0]).astype(BF16)
    vb[...] = v_ref[0].astype(BF16)


def _stack_heads(q, lo):
    qf = q.astype(F32)
    return jnp.concatenate([jnp.where(lo, qf, 0.0), jnp.where(lo, 0.0, qf)], axis=0).astype(BF16)


def _unstack_heads(o2, lo):
    t = o2.shape[0] // 2
    return jnp.where(lo, o2[:t], o2[t:])


def _softmax_pv(pieces):
    m = None
    for x, _ in pieces:
        mx = jnp.max(x, axis=-1, keepdims=True)
        m = mx if m is None else jnp.maximum(m, mx)
    l, o = None, None
    for x, vv in pieces:
        p = jnp.exp(x - m)
        ls = jnp.sum(p, axis=-1, keepdims=True)
        os_ = jnp.dot(p.astype(BF16), vv, preferred_element_type=F32)
        l = ls if l is None else l + ls
        o = os_ if o is None else o + os_
    return o / l


def _stacked_causal(t):
    r = lax.broadcasted_iota(jnp.int32, (2 * t, t), 0)
    r = jnp.where(r >= t, r - t, r)
    c = lax.broadcasted_iota(jnp.int32, (2 * t, t), 1)
    return c <= r, c < r


def _attn_a_prompt_kernel(q_ref, k_ref, v_ref, bias_ref, o_ref, kt, vb, *, nq, win, ncase):
    lo = _lo_mask()
    _prep_kv(k_ref, v_ref, kt, vb)
    for i in range(nq):
        q0 = i * A_TQ
        case = min(i, ncase - 1)
        w0 = max(q0 + A_TQ - win, 0)
        wlen = min(win, q0 + A_TQ)
        q2 = _stack_heads(q_ref[0, q0:q0 + A_TQ, :], lo)
        x = jnp.dot(q2, kt[:, w0:w0 + wlen], preferred_element_type=F32) + bias_ref[0, case, :, :wlen]
        o2 = _softmax_pv([(x, vb[w0:w0 + wlen, :])])
        o_ref[0, q0:q0 + A_TQ, :] = _unstack_heads(o2, lo).astype(o_ref.dtype)


def _toeplitz_bias(rel_bias, rows, cols, shift):
    p = rows + cols - 1
    idx = np.clip(np.arange(p) - (rows - 1) + shift, -REL_CLIP, REL_CLIP) + REL_CLIP
    e = jnp.transpose(rel_bias[idx]).astype(F32)
    h = e.shape[0]
    skew = jnp.tile(e, (1, rows + 1))[:, :rows * (p + 1)].reshape(h, rows, p + 1)
    return skew[:, ::-1, :cols]


def _band_bias_prompt(rel_bias, s_len):
    win = min(A_WIN, s_len)
    nq = s_len // A_TQ
    ncase = min(nq, A_PAST // A_TQ + 1)
    r = np.arange(A_TQ)[:, None]
    j = np.arange(win)[None, :]
    tabs = []
    for c in range(ncase):
        q_off = c * A_TQ if (c + 1) * A_TQ <= win else win - A_TQ
        qc, kc = (q_off + r) // CHUNK, j // CHUNK
        ok = (kc <= qc) & (kc >= qc - A_LEFT_CHUNKS)
        tabs.append(jnp.where(ok[None], _toeplitz_bias(rel_bias, A_TQ, win, -q_off), NEG))
    tab = jnp.stack(tabs, axis=1)
    h = tab.shape[0]
    tab = jnp.transpose(tab.reshape(h // 2, 2, ncase, A_TQ, win), (0, 2, 1, 3, 4))
    return tab.reshape(h // 2, ncase, 2 * A_TQ, win), win, ncase


def _attn_a_prompt(q, k, v, rel_bias):
    b, s, da = q.shape
    pairs = da // LANES
    bias, win, ncase = _band_bias_prompt(rel_bias, s)
    nq = s // A_TQ
    blk = lambda: pl.BlockSpec((1, s, LANES), lambda bi, p: (bi, 0, p))
    return pl.pallas_call(
        functools.partial(_attn_a_prompt_kernel, nq=nq, win=win, ncase=ncase),
        grid=(b, pairs),
        in_specs=[blk(), blk(), blk(),
                  pl.BlockSpec((1, ncase, 2 * A_TQ, win), lambda bi, p: (p, 0, 0, 0))],
        out_specs=blk(),
        out_shape=jax.ShapeDtypeStruct((b, s, da), BF16),
        scratch_shapes=[pltpu.VMEM((LANES, s), BF16), pltpu.VMEM((s, LANES), BF16)],
        compiler_params=_params(2, VMEM_LIMIT),
        name="attn_a_prompt",
    )(q, k, v, bias)


def _attn_a_sample_kernel(q_ref, kc_ref, vc_ref, kn_ref, vn_ref, bc_ref, bn_ref, o_ref):
    lo = _lo_mask()
    q2 = _stack_heads(q_ref[0], lo)
    xc = _dot_nt(q2, kc_ref[0].astype(BF16)) + bc_ref[0]
    xn = _dot_nt(q2, kn_ref[0].astype(BF16)) + bn_ref[0]
    o2 = _softmax_pv([(xc, vc_ref[0].astype(BF16)), (xn, vn_ref[0].astype(BF16))])
    o_ref[0] = _unstack_heads(o2, lo).astype(o_ref.dtype)


def _attn_a_sample(q, k_new, v_new, k_cache, v_cache, rel_bias, past):
    b, t, da = q.shape
    pairs = da // LANES
    n_band = k_cache.shape[1]
    q_pos = past + np.arange(t)[:, None]
    k_pos = past - n_band + np.arange(n_band + t)[None, :]
    qc, kc = q_pos // CHUNK, k_pos // CHUNK
    ok = (k_pos >= 0) & (kc <= qc) & (kc >= qc - A_LEFT_CHUNKS)
    tab = jnp.where(ok[None], _toeplitz_bias(rel_bias, t, n_band + t, -n_band), NEG)
    tab = tab.reshape(pairs, 2 * t, n_band + t)
    bias_c, bias_n = tab[..., :n_band], tab[..., n_band:]
    new = lambda: pl.BlockSpec((1, t, LANES), lambda bi, p: (bi, 0, p))
    old = lambda: pl.BlockSpec((1, n_band, LANES), lambda bi, p: (bi, 0, p))
    return pl.pallas_call(
        _attn_a_sample_kernel,
        grid=(b, pairs),
        in_specs=[new(), old(), old(), new(), new(),
                  pl.BlockSpec((1, 2 * t, n_band), lambda bi, p: (p, 0, 0)),
                  pl.BlockSpec((1, 2 * t, t), lambda bi, p: (p, 0, 0))],
        out_specs=new(),
        out_shape=jax.ShapeDtypeStruct((b, t, da), BF16),
        compiler_params=_params(2, VMEM_LIMIT),
        name="attn_a_sample",
    )(q, k_cache, v_cache, k_new, v_new, bias_c, bias_n)


def _attn_b_kernel(*refs, tq, nq, plen):
    if plen:
        (q_ref, k_ref, v_ref, crow_ref, ccol_ref, kp_ref, vp_ref, cpre_ref, o_ref, kt, vb, kpt, vpb) = refs
    else:
        q_ref, k_ref, v_ref, crow_ref, ccol_ref, o_ref, kt, vb = refs
    pair = pl.program_id(1)
    lo = _lo_mask()
    _prep_kv(k_ref, v_ref, kt, vb)
    if plen:
        _prep_kv(kp_ref, vp_ref, kpt, vpb)
    keep, _ = _stacked_causal(tq)
    head_lane = lax.broadcasted_iota(jnp.int32, (1, ccol_ref.shape[2]), 1)

    def logits(q2, kt_cols, cq, ck_ref, c0, c1):
        z = jnp.dot(q2, kt_cols, preferred_element_type=F32)
        return jnp.concatenate([z[:tq] + (cq[0] - ck_ref[0, 0, 0:1, c0:c1]),
                                z[tq:] + (cq[1] - ck_ref[0, 0, 1:2, c0:c1])], axis=0)

    for i in range(nq):
        d0, d1 = i * tq, (i + 1) * tq
        q2 = _stack_heads(q_ref[0, d0:d1, :], lo)
        cq_all = ccol_ref[0, d0:d1, :]
        cq = [jnp.sum(jnp.where(head_lane == 2 * pair + e, cq_all, 0.0), axis=-1, keepdims=True)
              for e in range(2)]
        pieces = []
        if plen:
            pieces.append((logits(q2, kpt[...], cq, cpre_ref, 0, plen), vpb[...]))
        if i:
            pieces.append((logits(q2, kt[:, 0:d0], cq, crow_ref, 0, d0), vb[0:d0, :]))
        xd = logits(q2, kt[:, d0:d1], cq, crow_ref, d0, d1)
        pieces.append((jnp.where(keep, xd, NEG), vb[d0:d1, :]))
        o_ref[0, d0:d1, :] = _unstack_heads(_softmax_pv(pieces), lo).astype(o_ref.dtype)


def _attn_b(q, k, v, c_self, k_pre=None, v_pre=None, c_pre=None):
    b, t, db = q.shape
    pairs = db // LANES
    n_heads = c_self.shape[1]
    tq = min(ATT_T, t)
    c_col = jnp.transpose(c_self, (0, 2, 1))
    new = lambda: pl.BlockSpec((1, t, LANES), lambda bi, p: (bi, 0, p))
    in_specs = [new(), new(), new(),
                pl.BlockSpec((1, 1, 2, t), lambda bi, p: (bi, p, 0, 0)),
                pl.BlockSpec((1, t, n_heads), lambda bi, p: (bi, 0, 0))]
    args = [q, k, v, c_self.reshape(b, pairs, 2, t), c_col]
    scratch = [pltpu.VMEM((LANES, t), BF16), pltpu.VMEM((t, LANES), BF16)]
    plen = 0
    if k_pre is not None:
        plen = k_pre.shape[1]
        old = lambda: pl.BlockSpec((1, plen, LANES), lambda bi, p: (bi, 0, p))
        in_specs += [old(), old(), pl.BlockSpec((1, 1, 2, plen), lambda bi, p: (bi, p, 0, 0))]
        args += [k_pre, v_pre, c_pre.reshape(b, pairs, 2, plen)]
        scratch += [pltpu.VMEM((LANES, plen), BF16), pltpu.VMEM((plen, LANES), BF16)]
    return pl.pallas_call(
        functools.partial(_attn_b_kernel, tq=tq, nq=t // tq, plen=plen),
        grid=(b, pairs),
        in_specs=in_specs,
        out_specs=new(),
        out_shape=jax.ShapeDtypeStruct((b, t, db), BF16),
        scratch_shapes=scratch,
        compiler_params=_params(2, VMEM_LIMIT),
        name="attn_b",
    )(*args)


def _split2(x):
    hi = x.astype(BF16)
    return hi, (x - hi.astype(F32)).astype(BF16)


def _later_ones(t):
    r = lax.broadcasted_iota(jnp.int32, (t, t), 0)
    c = lax.broadcasted_iota(jnp.int32, (t, t), 1)
    return jnp.where(r > c, 1.0, 0.0).astype(BF16)


def _stick_weights(z, blk, ones_ref, carry, before):
    rows = z.shape[0]
    soft = jnp.maximum(z, 0.0) + jnp.log(1.0 + jnp.exp(-jnp.abs(z)))
    nblk = z.shape[1] // blk
    ws = [None] * nblk
    for j in reversed(range(nblk)):
        cols = slice(j * blk, (j + 1) * blk)
        sp = soft[:, cols]
        masked = before is not None and j == nblk - 1
        keep = jnp.where(before, sp, 0.0) if masked else sp
        both = jnp.dot(jnp.concatenate(_split2(keep), axis=0), ones_ref[...], preferred_element_type=F32)
        inner = both[:rows] + both[rows:]
        w = jnp.exp(z[:, cols] - sp - (inner + carry))
        if masked:
            w = jnp.where(before, w, 0.0)
        ws[j] = w.astype(BF16)
        carry = carry + (inner[:, 0:1] + keep[:, 0:1])
    return (ws[0] if nblk == 1 else jnp.concatenate(ws, axis=-1)), carry


def _attn_c_kernel(*refs, tq, nq, plen, tpre):
    if plen:
        q_ref, k_ref, v_ref, kp_ref, vp_ref, o_ref, kt, vb, u_self, kpt, vpb, u_pre = refs
    else:
        q_ref, k_ref, v_ref, o_ref, kt, vb, u_self = refs
    lo = _lo_mask()
    _prep_kv(k_ref, v_ref, kt, vb)
    u_self[...] = _later_ones(tq)
    if plen:
        _prep_kv(kp_ref, vp_ref, kpt, vpb)
        u_pre[...] = _later_ones(tpre)
    _, before = _stacked_causal(tq)
    for i in range(nq):
        d0, d1 = i * tq, (i + 1) * tq
        q2 = _stack_heads(q_ref[0, d0:d1, :], lo)
        z = jnp.dot(q2, kt[:, 0:d1], preferred_element_type=F32)
        w, carry = _stick_weights(z, tq, u_self, jnp.zeros((2 * tq, 1), F32), before)
        o2 = jnp.dot(w, vb[0:d1, :], preferred_element_type=F32)
        if plen:
            wp, _ = _stick_weights(jnp.dot(q2, kpt[...], preferred_element_type=F32), tpre, u_pre, carry, None)
            o2 = o2 + jnp.dot(wp, vpb[...], preferred_element_type=F32)
        o_ref[0, d0:d1, :] = _unstack_heads(o2, lo).astype(o_ref.dtype)


def _attn_c(q, k, v, k_pre=None, v_pre=None):
    b, t, dc = q.shape
    pairs = dc // LANES
    tq = min(ATT_T, t)
    new = lambda: pl.BlockSpec((1, t, LANES), lambda bi, p: (bi, 0, p))
    in_specs = [new(), new(), new()]
    args = [q, k, v]
    scratch = [pltpu.VMEM((LANES, t), BF16), pltpu.VMEM((t, LANES), BF16), pltpu.VMEM((tq, tq), BF16)]
    plen = tpre = 0
    if k_pre is not None:
        plen = k_pre.shape[1]
        tpre = min(ATT_T, plen)
        old = lambda: pl.BlockSpec((1, plen, LANES), lambda bi, p: (bi, 0, p))
        in_specs += [old(), old()]
        args += [k_pre, v_pre]
        scratch += [pltpu.VMEM((LANES, plen), BF16), pltpu.VMEM((plen, LANES), BF16),
                    pltpu.VMEM((tpre, tpre), BF16)]
    return pl.pallas_call(
        functools.partial(_attn_c_kernel, tq=tq, nq=t // tq, plen=plen, tpre=tpre),
        grid=(b, pairs),
        in_specs=in_specs,
        out_specs=new(),
        out_shape=jax.ShapeDtypeStruct((b, t, dc), BF16),
        scratch_shapes=scratch,
        compiler_params=_params(2, VMEM_LIMIT),
        name="attn_c",
    )(*args)


def _softmax_rows(x):
    m = jnp.max(x, axis=-1, keepdims=True)
    p = jnp.exp(x - m)
    return p, jnp.sum(p, axis=-1, keepdims=True)


def _head_norm_x(y, gain):
    parts = []
    for h in range(N_HEADS_X):
        blk = y[:, h * HEAD_DIM_X:(h + 1) * HEAD_DIM_X]
        ms = jnp.mean(blk * blk, axis=-1, keepdims=True)
        parts.append(blk * lax.rsqrt(ms + EPS) * gain)
    return parts


def _mem_kv_kernel(m_ref, g_ref, w_ref, gain_ref, k_ref, v_ref, kb_ref, vb_ref):
    xn = _rms_rows(m_ref[...], g_ref[...]).astype(BF16)
    dx = w_ref.shape[1] // 2
    k = jnp.dot(xn, w_ref[:, :dx], preferred_element_type=F32)
    for h, blk in enumerate(_head_norm_x(k, gain_ref[...])):
        k_ref[:, h * HEAD_DIM_X:(h + 1) * HEAD_DIM_X] = blk
        kb_ref[:, h * HEAD_DIM_X:(h + 1) * HEAD_DIM_X] = blk.astype(BF16)
    v = jnp.dot(xn, w_ref[:, dx:], preferred_element_type=F32)
    v_ref[...] = v
    vb_ref[...] = v.astype(BF16)


def _mem_kv(mem2d, g, w_kv, k_gain):
    n = mem2d.shape[0]
    ts = _row_tile(n)
    dx = w_kv.shape[1] // 2
    row = lambda width: pl.BlockSpec((ts, width), lambda i: (i, 0))
    return pl.pallas_call(
        _mem_kv_kernel,
        grid=(n // ts,),
        in_specs=[row(D_MODEL), _const_spec(g.shape), _const_spec(w_kv.shape), _const_spec(k_gain.shape)],
        out_specs=[row(dx)] * 4,
        out_shape=(jax.ShapeDtypeStruct((n, dx), F32), jax.ShapeDtypeStruct((n, dx), F32),
                   jax.ShapeDtypeStruct((n, dx), BF16), jax.ShapeDtypeStruct((n, dx), BF16)),
        compiler_params=_params(1, VMEM_LIMIT),
        name="mem_kv",
    )(mem2d, g, w_kv, k_gain)


def _post_kernel(h_ref, o1_ref, o2_ref, wout_ref, g_ref, wq_ref, qgain_ref, mk_ref, mv_ref, wo_ref, y_ref):
    half = o1_ref.shape[2]
    h1 = (h_ref[0]
          + jnp.dot(o1_ref[0], wout_ref[:half, :], preferred_element_type=F32)
          + jnp.dot(o2_ref[0], wout_ref[half:, :], preferred_element_type=F32))
    hn = _rms_rows(h1, g_ref[...]).astype(BF16)
    qx = jnp.dot(hn, wq_ref[...], preferred_element_type=F32)
    heads = []
    for h, qh in enumerate(_head_norm_x(qx, qgain_ref[...])):
        cols = slice(h * HEAD_DIM_X, (h + 1) * HEAD_DIM_X)
        s = _dot_nt((qh * X_SCALE).astype(BF16), mk_ref[0, :, cols])
        p, l = _softmax_rows(s)
        oh = jnp.dot(p.astype(BF16), mv_ref[0, :, cols], preferred_element_type=F32) / l
        heads.append(oh.astype(BF16))
    y_ref[0] = h1 + jnp.dot(jnp.concatenate(heads, axis=-1), wo_ref[...], preferred_element_type=F32)


def _post(h, o1, o2, w_out, g_cross, w_q, q_gain, mk, mv, w_o):
    b, s, d = h.shape
    ts = _row_tile(s)
    half = w_out.shape[0] // 2
    n_mem = mk.shape[1]
    tile = lambda width: pl.BlockSpec((1, ts, width), lambda bi, i: (bi, i, 0))
    o1_spec = pl.BlockSpec((1, ts, half), lambda bi, i: (bi, i, 0))
    if o2 is None:
        o2 = o1
        o2_spec = pl.BlockSpec((1, ts, half), lambda bi, i: (bi, i, 1))
    else:
        o2_spec = o1_spec
    mem = lambda: pl.BlockSpec((1, n_mem, d), lambda bi, i: (bi, 0, 0))
    return pl.pallas_call(
        _post_kernel,
        grid=(b, s // ts),
        in_specs=[tile(d), o1_spec, o2_spec, _const_spec(w_out.shape), _const_spec(g_cross.shape),
                  _const_spec(w_q.shape), _const_spec(q_gain.shape), mem(), mem(), _const_spec(w_o.shape)],
        out_specs=tile(d),
        out_shape=jax.ShapeDtypeStruct((b, s, d), F32),
        compiler_params=_params(2, VMEM_LIMIT),
        name="post_mixer_cross",
    )(h, o1, o2, w_out, g_cross, w_q, q_gain, mk, mv, w_o)


def _mlp_kernel(h_ref, g_ref, wup_ref, wdn_ref, y_ref):
    h = h_ref[...]
    hn = _rms_rows(h, g_ref[...]).astype(BF16)
    acc = h
    for c in range(wup_ref.shape[1] // FF_CHUNK):
        cols = slice(c * FF_CHUNK, (c + 1) * FF_CHUNK)
        u = jnp.maximum(jnp.dot(hn, wup_ref[:, cols], preferred_element_type=F32), 0.0)
        acc = acc + jnp.dot((u * u).astype(BF16), wdn_ref[cols, :], preferred_element_type=F32)
    y_ref[...] = acc


def _mlp(h2d, g, w_up, w_down):
    n = h2d.shape[0]
    ts = _row_tile(n)
    row = pl.BlockSpec((ts, D_MODEL), lambda i: (i, 0))
    return pl.pallas_call(
        _mlp_kernel,
        grid=(n // ts,),
        in_specs=[row, _const_spec(g.shape), _const_spec(w_up.shape), _const_spec(w_down.shape)],
        out_specs=row,
        out_shape=jax.ShapeDtypeStruct((n, D_MODEL), F32),
        compiler_params=_params(1, VMEM_LIMIT),
        name="mlp",
    )(h2d, g, w_up, w_down)


def _row_vec(x):
    return x.reshape(1, -1).astype(F32)


def _ab_weights(w_in, b_f, qk_gain):
    dab3 = w_in.shape[1] - b_f.shape[0]
    n_f = b_f.shape[0]
    w_main = w_in[:, :dab3].astype(BF16)
    wf = jnp.zeros((w_in.shape[0], LANES), F32).at[:, :n_f].set(w_in[:, dab3:]).astype(BF16)
    bf = jnp.zeros((1, LANES), F32).at[0, :n_f].set(b_f)
    gains = jnp.concatenate([qk_gain, qk_gain], axis=-1).astype(F32)
    return w_main, wf, bf, gains


def _heads_first(x):
    return jnp.transpose(x, (0, 2, 1))


def kernel(x_prompt, x_sample, mem_prompt, cache_a_k, cache_a_v, cache_b_k, cache_b_v, cache_b_logf,
           cache_c_k, cache_c_v, cache_mem_k, cache_mem_v, norm_mix, norm_cross, norm_mlp, norm_mem,
           ab_w_in, ab_forget_bias, ab_qk_gain, ab_rel_bias, ab_w_out, sb_w_in, sb_w_out,
           x_w_q, x_w_kv, x_qk_gain, x_w_o, mlp_w_up, mlp_w_down):
    bp, sp, d = x_prompt.shape
    bs, ss, _ = x_sample.shape
    depth = norm_mix.shape[0]
    n_mem = mem_prompt.shape[1]
    hb = ab_forget_bias.shape[1]
    hp, hs = x_prompt, x_sample
    outs = {name: [] for name in ("a_kp", "a_vp", "a_ks", "a_vs", "b_kp", "b_vp", "b_fp", "b_ks", "b_vs",
                                  "b_fs", "c_kp", "c_vp", "c_ks", "c_vs", "m_kp", "m_vp")}
    mem2d = mem_prompt.reshape(bp * n_mem, d)

    for layer in range(depth):
        i = layer // 2
        g_mix = _row_vec(norm_mix[layer])
        if layer % 2 == 0:
            w_main, wf, bf, gains = _ab_weights(ab_w_in[i], ab_forget_bias[i], ab_qk_gain[i])
            da = w_main.shape[1] // 6
            ha = da // HEAD_DIM
            qa, ka, va, qb, kb, vb, lf = _proj_ab(hp.reshape(bp * sp, d), g_mix, w_main, wf, bf, gains)
            r3 = lambda x, b_, s_: x.reshape(b_, s_, x.shape[-1])
            qa, ka, va, qb, kb, vb = (r3(x, bp, sp) for x in (qa, ka, va, qb, kb, vb))
            logf = r3(lf, bp, sp)[..., :hb]
            c_p = _cumsum_rows(_heads_first(logf).reshape(bp * hb, sp)).reshape(bp, hb, sp)
            oa = _attn_a_prompt(qa, ka, va, ab_rel_bias[i])
            ob = _attn_b(qb, kb, vb, c_p)
            keep = min(A_PAST, sp)
            outs["a_kp"].append(ka[:, sp - keep:].reshape(bp, keep, ha, HEAD_DIM))
            outs["a_vp"].append(va[:, sp - keep:].reshape(bp, keep, ha, HEAD_DIM))
            outs["b_kp"].append(kb.reshape(bp, sp, hb, HEAD_DIM))
            outs["b_vp"].append(vb.reshape(bp, sp, hb, HEAD_DIM))
            outs["b_fp"].append(logf)
            mix_p = (oa, ob)
            qa, ka, va, qb, kb, vb, lf = _proj_ab(hs.reshape(bs * ss, d), g_mix, w_main, wf, bf, gains)
            qa, ka, va, qb, kb, vb = (r3(x, bs, ss) for x in (qa, ka, va, qb, kb, vb))
            logf = r3(lf, bs, ss)[..., :hb]
            past = cache_b_k.shape[2]
            n_band = cache_a_k.shape[2]
            lf_all = jnp.concatenate([_heads_first(cache_b_logf[i].astype(F32)), _heads_first(logf)], axis=-1)
            total = past + ss
            padded = -(-total // LANES) * LANES
            lf_all = jnp.pad(lf_all, ((0, 0), (0, 0), (0, padded - total)))
            c_all = _cumsum_rows(lf_all.reshape(bs * hb, padded)).reshape(bs, hb, padded)
            oa = _attn_a_sample(qa, ka, va, cache_a_k[i].reshape(bs, n_band, da),
                                cache_a_v[i].reshape(bs, n_band, da), ab_rel_bias[i], past)
            ob = _attn_b(qb, kb, vb, c_all[..., past:total],
                         cache_b_k[i].reshape(bs, past, da), cache_b_v[i].reshape(bs, past, da),
                         c_all[..., :past])
            outs["a_ks"].append(ka.reshape(bs, ss, ha, HEAD_DIM))
            outs["a_vs"].append(va.reshape(bs, ss, ha, HEAD_DIM))
            outs["b_ks"].append(kb.reshape(bs, ss, hb, HEAD_DIM))
            outs["b_vs"].append(vb.reshape(bs, ss, hb, HEAD_DIM))
            outs["b_fs"].append(logf)
            mix_s = (oa, ob)
            w_out = ab_w_out[i].astype(BF16)
        else:
            w_sb = sb_w_in[i].astype(BF16)
            dc = w_sb.shape[1] // 3
            hc = dc // HEAD_DIM
            q, k, v = _proj_sb(hp.reshape(bp * sp, d), g_mix, w_sb)
            q, k, v = (x.reshape(bp, sp, dc) for x in (q, k, v))
            mix_p = (_attn_c(q, k, v), None)
            outs["c_kp"].append(k.reshape(bp, sp, hc, HEAD_DIM))
            outs["c_vp"].append(v.reshape(bp, sp, hc, HEAD_DIM))
            q, k, v = _proj_sb(hs.reshape(bs * ss, d), g_mix, w_sb)
            q, k, v = (x.reshape(bs, ss, dc) for x in (q, k, v))
            past = cache_c_k.shape[2]
            mix_s = (_attn_c(q, k, v, cache_c_k[i].reshape(bs, past, dc), cache_c_v[i].reshape(bs, past, dc)),
                     None)
            outs["c_ks"].append(k.reshape(bs, ss, hc, HEAD_DIM))
            outs["c_vs"].append(v.reshape(bs, ss, hc, HEAD_DIM))
            w_out = sb_w_out[i].astype(BF16)

        mk, mv, mkb, mvb = _mem_kv(mem2d, _row_vec(norm_mem[layer]), x_w_kv[layer].astype(BF16),
                                   _row_vec(x_qk_gain[layer, 1]))
        outs["m_kp"].append(mk.reshape(bp, n_mem, N_HEADS_X, HEAD_DIM_X))
        outs["m_vp"].append(mv.reshape(bp, n_mem, N_HEADS_X, HEAD_DIM_X))
        g_cross = _row_vec(norm_cross[layer])
        w_q, w_o = x_w_q[layer].astype(BF16), x_w_o[layer].astype(BF16)
        q_gain = _row_vec(x_qk_gain[layer, 0])
        hp = _post(hp, mix_p[0], mix_p[1], w_out, g_cross, w_q, q_gain,
                   mkb.reshape(bp, n_mem, d), mvb.reshape(bp, n_mem, d), w_o)
        hs = _post(hs, mix_s[0], mix_s[1], w_out, g_cross, w_q, q_gain,
                   cache_mem_k[layer].reshape(bs, n_mem, d).astype(BF16),
                   cache_mem_v[layer].reshape(bs, n_mem, d).astype(BF16), w_o)
        g_mlp = _row_vec(norm_mlp[layer])
        w_up, w_dn = mlp_w_up[layer].astype(BF16), mlp_w_down[layer].astype(BF16)
        hp = _mlp(hp.reshape(bp * sp, d), g_mlp, w_up, w_dn).reshape(bp, sp, d)
        hs = _mlp(hs.reshape(bs * ss, d), g_mlp, w_up, w_dn).reshape(bs, ss, d)

    st = lambda name: jnp.stack(outs[name])
    return (hp, hs, st("a_kp"), st("a_vp"), st("a_ks"), st("a_vs"),
            st("b_kp"), st("b_vp"), st("b_fp"), st("b_ks"), st("b_vs"), st("b_fs"),
            st("c_kp"), st("c_vp"), st("c_ks"), st("c_vs"), st("m_kp"), st("m_vp"))
```

```python
import functools

import numpy as np
import jax
import jax.numpy as jnp
from jax import lax
from jax.experimental import pallas as pl
from jax.experimental.pallas import tpu as pltpu

F32 = jnp.float32
BF16 = jnp.bfloat16

D_MODEL = 1024
HEAD_DIM = 64
LANES = 128
CHUNK = 64
A_LEFT_CHUNKS = 8
A_PAST = A_LEFT_CHUNKS * CHUNK
REL_CLIP = 128
N_HEADS_X = 4
HEAD_DIM_X = D_MODEL // N_HEADS_X
D_FF = 4 * D_MODEL
EPS = 1e-6
NEG = -1e30
QK_SCALE = HEAD_DIM ** -0.5
X_SCALE = HEAD_DIM_X ** -0.5

A_TQ = 128
A_WIN = A_PAST + A_TQ
ATT_T = 256
FF_CHUNK = 1024
VMEM_LIMIT = 56 * 1024 * 1024


def _row_tile(n, cap=512):
    t = cap
    while n % t:
        t //= 2
    return t


def _params(n_axes, vmem=None):
    return pltpu.CompilerParams(dimension_semantics=("arbitrary",) * n_axes,
                                vmem_limit_bytes=vmem)


def _const_spec(shape):
    nd = len(shape)
    return pl.BlockSpec(shape, lambda *_: (0,) * nd, pipeline_mode=pl.Buffered(1))


def _rms_rows(x, g):
    ms = jnp.mean(x * x, axis=-1, keepdims=True)
    return x * lax.rsqrt(ms + EPS) * g


def _log_sigmoid(x):
    return jnp.minimum(x, 0.0) - jnp.log1p(jnp.exp(-jnp.abs(x)))


def _lo_mask():
    return lax.broadcasted_iota(jnp.int32, (1, LANES), 1) < HEAD_DIM


def _pair_headnorm(blk, gain, lo):
    sq = blk * blk
    s_lo = jnp.sum(jnp.where(lo, sq, 0.0), axis=-1, keepdims=True)
    s_hi = jnp.sum(jnp.where(lo, 0.0, sq), axis=-1, keepdims=True)
    inv = jnp.where(lo, lax.rsqrt(s_lo * (1.0 / HEAD_DIM) + EPS),
                    lax.rsqrt(s_hi * (1.0 / HEAD_DIM) + EPS))
    return blk * inv * gain


def _dot_nt(a, b):
    return lax.dot_general(a, b, (((1,), (1,)), ((), ())), preferred_element_type=F32)


def _proj_ab_kernel(h_ref, g_ref, w_ref, wf_ref, bf_ref, gain_ref,
                    qa_ref, ka_ref, va_ref, qb_ref, kb_ref, vb_ref, lf_ref):
    xn = _rms_rows(h_ref[...], g_ref[...]).astype(BF16)
    lo = _lo_mask()
    seg_w = w_ref.shape[1] // 6
    outs = ((qa_ref, 0, True), (ka_ref, 1, False), (va_ref, None, False),
            (qb_ref, 2, True), (kb_ref, 3, False), (vb_ref, None, False))
    for s, (ref, gi, is_q) in enumerate(outs):
        y = jnp.dot(xn, w_ref[:, s * seg_w:(s + 1) * seg_w], preferred_element_type=F32)
        if gi is None:
            ref[...] = y
            continue
        gain = gain_ref[gi:gi + 1, :]
        for p in range(seg_w // LANES):
            blk = _pair_headnorm(y[:, p * LANES:(p + 1) * LANES], gain, lo)
            if is_q:
                blk = blk * QK_SCALE
            ref[:, p * LANES:(p + 1) * LANES] = blk.astype(ref.dtype)
    fl = jnp.dot(xn, wf_ref[...], preferred_element_type=F32) + bf_ref[...]
    lf_ref[...] = _log_sigmoid(fl)


def _proj_ab(h2d, g, w_main, wf, bf, gains):
    n = h2d.shape[0]
    ts = _row_tile(n)
    dab = w_main.shape[1] // 6
    row = lambda w: pl.BlockSpec((ts, w), lambda i: (i, 0))
    out_shape = (jax.ShapeDtypeStruct((n, dab), BF16), jax.ShapeDtypeStruct((n, dab), F32),
                 jax.ShapeDtypeStruct((n, dab), F32), jax.ShapeDtypeStruct((n, dab), BF16),
                 jax.ShapeDtypeStruct((n, dab), F32), jax.ShapeDtypeStruct((n, dab), F32),
                 jax.ShapeDtypeStruct((n, LANES), F32))
    return pl.pallas_call(
        _proj_ab_kernel,
        grid=(n // ts,),
        in_specs=[row(D_MODEL), _const_spec(g.shape), _const_spec(w_main.shape),
                  _const_spec(wf.shape), _const_spec(bf.shape), _const_spec(gains.shape)],
        out_specs=[row(dab)] * 6 + [row(LANES)],
        out_shape=out_shape,
        compiler_params=_params(1, VMEM_LIMIT),
        name="proj_ab",
    )(h2d, g, w_main, wf, bf, gains)


def _proj_sb_kernel(h_ref, g_ref, w_ref, q_ref, k_ref, v_ref):
    xn = _rms_rows(h_ref[...], g_ref[...]).astype(BF16)
    dc = w_ref.shape[1] // 3
    q = jnp.dot(xn, w_ref[:, :dc], preferred_element_type=F32)
    q_ref[...] = (q * QK_SCALE).astype(BF16)
    k_ref[...] = jnp.dot(xn, w_ref[:, dc:2 * dc], preferred_element_type=F32)
    v_ref[...] = jnp.dot(xn, w_ref[:, 2 * dc:], preferred_element_type=F32)


def _proj_sb(h2d, g, w):
    n = h2d.shape[0]
    ts = _row_tile(n)
    dc = w.shape[1] // 3
    row = lambda width: pl.BlockSpec((ts, width), lambda i: (i, 0))
    return pl.pallas_call(
        _proj_sb_kernel,
        grid=(n // ts,),
        in_specs=[row(D_MODEL), _const_spec(g.shape), _const_spec(w.shape)],
        out_specs=[row(dc)] * 3,
        out_shape=(jax.ShapeDtypeStruct((n, dc), BF16), jax.ShapeDtypeStruct((n, dc), F32),
                   jax.ShapeDtypeStruct((n, dc), F32)),
        compiler_params=_params(1, VMEM_LIMIT),
        name="proj_sb",
    )(h2d, g, w)


def _split3(x):
    hi = x.astype(BF16)
    r = x - hi.astype(F32)
    mid = r.astype(BF16)
    lo = (r - mid.astype(F32)).astype(BF16)
    return hi, mid, lo


def _cumsum_kernel(x_ref, o_ref):
    rows, length = x_ref.shape
    r_i = lax.broadcasted_iota(jnp.int32, (LANES, LANES), 0)
    c_i = lax.broadcasted_iota(jnp.int32, (LANES, LANES), 1)
    tri = jnp.where(r_i <= c_i, 1.0, 0.0).astype(BF16)
    carry = jnp.zeros((rows, 1), F32)
    for b in range(length // LANES):
        hi, mid, lo = _split3(x_ref[:, b * LANES:(b + 1) * LANES])
        c = (jnp.dot(hi, tri, preferred_element_type=F32)
             + jnp.dot(mid, tri, preferred_element_type=F32)
             + jnp.dot(lo, tri, preferred_element_type=F32)) + carry
        o_ref[:, b * LANES:(b + 1) * LANES] = c
        carry = c[:, LANES - 1:LANES]


def _cumsum_rows(x):
    return pl.pallas_call(
        _cumsum_kernel,
        out_shape=jax.ShapeDtypeStruct(x.shape, F32),
        name="cumsum_logf",
    )(x)


def _prep_kv(k_ref, v_ref, kt, vb):
    kt[...] = jnp.transpose(k_ref[0]).astype(BF16)
    vb[...] = v_ref[0].astype(BF16)


def _stack_heads(q, lo):
    qf = q.astype(F32)
    return jnp.concatenate([jnp.where(lo, qf, 0.0), jnp.where(lo, 0.0, qf)], axis=0).astype(BF16)


def _unstack_heads(o2, lo):
    t = o2.shape[0] // 2
    return jnp.where(lo, o2[:t], o2[t:])


def _softmax_pv(pieces):
    m = None
    for x, _ in pieces:
        mx = jnp.max(x, axis=-1, keepdims=True)
        m = mx if m is None else jnp.maximum(m, mx)
    l, o = None, None
    for x, vv in pieces:
        p = jnp.exp(x - m)
        ls = jnp.sum(p, axis=-1, keepdims=True)
        os_ = jnp.dot(p.astype(BF16), vv, preferred_element_type=F32)
        l = ls if l is None else l + ls
        o = os_ if o is None else o + os_
    return o / l


def _stacked_causal(t):
    r = lax.broadcasted_iota(jnp.int32, (2 * t, t), 0)
    r = jnp.where(r >= t, r - t, r)
    c = lax.broadcasted_iota(jnp.int32, (2 * t, t), 1)
    return c <= r, c < r


def _attn_a_prompt_kernel(q_ref, k_ref, v_ref, bias_ref, o_ref, kt, vb, *, nq, win, ncase):
    lo = _lo_mask()
    _prep_kv(k_ref, v_ref, kt, vb)
    for i in range(nq):
        q0 = i * A_TQ
        case = min(i, ncase - 1)
        w0 = max(q0 + A_TQ - win, 0)
        wlen = min(win, q0 + A_TQ)
        q2 = _stack_heads(q_ref[0, q0:q0 + A_TQ, :], lo)
        x = jnp.dot(q2, kt[:, w0:w0 + wlen], preferred_element_type=F32) + bias_ref[0, case, :, :wlen]
        o2 = _softmax_pv([(x, vb[w0:w0 + wlen, :])])
        o_ref[0, q0:q0 + A_TQ, :] = _unstack_heads(o2, lo).astype(o_ref.dtype)


def _toeplitz_bias(rel_bias, rows, cols, shift):
    p = rows + cols - 1
    idx = np.clip(np.arange(p) - (rows - 1) + shift, -REL_CLIP, REL_CLIP) + REL_CLIP
    e = jnp.transpose(rel_bias[idx]).astype(F32)
    h = e.shape[0]
    skew = jnp.tile(e, (1, rows + 1))[:, :rows * (p + 1)].reshape(h, rows, p + 1)
    return skew[:, ::-1, :cols]


def _band_bias_prompt(rel_bias, s_len):
    win = min(A_WIN, s_len)
    nq = s_len // A_TQ
    ncase = min(nq, A_PAST // A_TQ + 1)
    r = np.arange(A_TQ)[:, None]
    j = np.arange(win)[None, :]
    tabs = []
    for c in range(ncase):
        q_off = c * A_TQ if (c + 1) * A_TQ <= win else win - A_TQ
        qc, kc = (q_off + r) // CHUNK, j // CHUNK
        ok = (kc <= qc) & (kc >= qc - A_LEFT_CHUNKS)
        tabs.append(jnp.where(ok[None], _toeplitz_bias(rel_bias, A_TQ, win, -q_off), NEG))
    tab = jnp.stack(tabs, axis=1)
    h = tab.shape[0]
    tab = jnp.transpose(tab.reshape(h // 2, 2, ncase, A_TQ, win), (0, 2, 1, 3, 4))
    return tab.reshape(h // 2, ncase, 2 * A_TQ, win), win, ncase


def _attn_a_prompt(q, k, v, rel_bias):
    b, s, da = q.shape
    pairs = da // LANES
    bias, win, ncase = _band_bias_prompt(rel_bias, s)
    nq = s // A_TQ
    blk = lambda: pl.BlockSpec((1, s, LANES), lambda bi, p: (bi, 0, p))
    return pl.pallas_call(
        functools.partial(_attn_a_prompt_kernel, nq=nq, win=win, ncase=ncase),
        grid=(b, pairs),
        in_specs=[blk(), blk(), blk(),
                  pl.BlockSpec((1, ncase, 2 * A_TQ, win), lambda bi, p: (p, 0, 0, 0))],
        out_specs=blk(),
        out_shape=jax.ShapeDtypeStruct((b, s, da), BF16),
        scratch_shapes=[pltpu.VMEM((LANES, s), BF16), pltpu.VMEM((s, LANES), BF16)],
        compiler_params=_params(2, VMEM_LIMIT),
        name="attn_a_prompt",
    )(q, k, v, bias)


def _attn_a_sample_kernel(q_ref, kc_ref, vc_ref, kn_ref, vn_ref, bc_ref, bn_ref, o_ref):
    lo = _lo_mask()
    q2 = _stack_heads(q_ref[0], lo)
    xc = _dot_nt(q2, kc_ref[0].astype(BF16)) + bc_ref[0]
    xn = _dot_nt(q2, kn_ref[0].astype(BF16)) + bn_ref[0]
    o2 = _softmax_pv([(xc, vc_ref[0].astype(BF16)), (xn, vn_ref[0].astype(BF16))])
    o_ref[0] = _unstack_heads(o2, lo).astype(o_ref.dtype)


def _attn_a_sample(q, k_new, v_new, k_cache, v_cache, rel_bias, past):
    b, t, da = q.shape
    pairs = da // LANES
    n_band = k_cache.shape[1]
    q_pos = past + np.arange(t)[:, None]
    k_pos = past - n_band + np.arange(n_band + t)[None, :]
    qc, kc = q_pos // CHUNK, k_pos // CHUNK
    ok = (k_pos >= 0) & (kc <= qc) & (kc >= qc - A_LEFT_CHUNKS)
    tab = jnp.where(ok[None], _toeplitz_bias(rel_bias, t, n_band + t, -n_band), NEG)
    tab = tab.reshape(pairs, 2 * t, n_band + t)
    bias_c, bias_n = tab[..., :n_band], tab[..., n_band:]
    new = lambda: pl.BlockSpec((1, t, LANES), lambda bi, p: (bi, 0, p))
    old = lambda: pl.BlockSpec((1, n_band, LANES), lambda bi, p: (bi, 0, p))
    return pl.pallas_call(
        _attn_a_sample_kernel,
        grid=(b, pairs),
        in_specs=[new(), old(), old(), new(), new(),
                  pl.BlockSpec((1, 2 * t, n_band), lambda bi, p: (p, 0, 0)),
                  pl.BlockSpec((1, 2 * t, t), lambda bi, p: (p, 0, 0))],
        out_specs=new(),
        out_shape=jax.ShapeDtypeStruct((b, t, da), BF16),
        compiler_params=_params(2, VMEM_LIMIT),
        name="attn_a_sample",
    )(q, k_cache, v_cache, k_new, v_new, bias_c, bias_n)


def _attn_b_kernel(*refs, tq, nq, plen):
    if plen:
        (q_ref, k_ref, v_ref, crow_ref, ccol_ref, kp_ref, vp_ref, cpre_ref, o_ref, kt, vb, kpt, vpb) = refs
    else:
        q_ref, k_ref, v_ref, crow_ref, ccol_ref, o_ref, kt, vb = refs
    pair = pl.program_id(1)
    lo = _lo_mask()
    _prep_kv(k_ref, v_ref, kt, vb)
    if plen:
        _prep_kv(kp_ref, vp_ref, kpt, vpb)
    keep, _ = _stacked_causal(tq)
    head_lane = lax.broadcasted_iota(jnp.int32, (1, ccol_ref.shape[2]), 1)

    def logits(q2, kt_cols, cq, ck_ref, c0, c1):
        z = jnp.dot(q2, kt_cols, preferred_element_type=F32)
        return jnp.concatenate([z[:tq] + (cq[0] - ck_ref[0, 0, 0:1, c0:c1]),
                                z[tq:] + (cq[1] - ck_ref[0, 0, 1:2, c0:c1])], axis=0)

    for i in range(nq):
        d0, d1 = i * tq, (i + 1) * tq
        q2 = _stack_heads(q_ref[0, d0:d1, :], lo)
        cq_all = ccol_ref[0, d0:d1, :]
        cq = [jnp.sum(jnp.where(head_lane == 2 * pair + e, cq_all, 0.0), axis=-1, keepdims=True)
              for e in range(2)]
        pieces = []
        if plen:
            pieces.append((logits(q2, kpt[...], cq, cpre_ref, 0, plen), vpb[...]))
        if i:
            pieces.append((logits(q2, kt[:, 0:d0], cq, crow_ref, 0, d0), vb[0:d0, :]))
        xd = logits(q2, kt[:, d0:d1], cq, crow_ref, d0, d1)
        pieces.append((jnp.where(keep, xd, NEG), vb[d0:d1, :]))
        o_ref[0, d0:d1, :] = _unstack_heads(_softmax_pv(pieces), lo).astype(o_ref.dtype)


def _attn_b(q, k, v, c_self, k_pre=None, v_pre=None, c_pre=None):
    b, t, db = q.shape
    pairs = db // LANES
    n_heads = c_self.shape[1]
    tq = min(ATT_T, t)
    c_col = jnp.transpose(c_self, (0, 2, 1))
    new = lambda: pl.BlockSpec((1, t, LANES), lambda bi, p: (bi, 0, p))
    in_specs = [new(), new(), new(),
                pl.BlockSpec((1, 1, 2, t), lambda bi, p: (bi, p, 0, 0)),
                pl.BlockSpec((1, t, n_heads), lambda bi, p: (bi, 0, 0))]
    args = [q, k, v, c_self.reshape(b, pairs, 2, t), c_col]
    scratch = [pltpu.VMEM((LANES, t), BF16), pltpu.VMEM((t, LANES), BF16)]
    plen = 0
    if k_pre is not None:
        plen = k_pre.shape[1]
        old = lambda: pl.BlockSpec((1, plen, LANES), lambda bi, p: (bi, 0, p))
        in_specs += [old(), old(), pl.BlockSpec((1, 1, 2, plen), lambda bi, p: (bi, p, 0, 0))]
        args += [k_pre, v_pre, c_pre.reshape(b, pairs, 2, plen)]
        scratch += [pltpu.VMEM((LANES, plen), BF16), pltpu.VMEM((plen, LANES), BF16)]
    return pl.pallas_call(
        functools.partial(_attn_b_kernel, tq=tq, nq=t // tq, plen=plen),
        grid=(b, pairs),
        in_specs=in_specs,
        out_specs=new(),
        out_shape=jax.ShapeDtypeStruct((b, t, db), BF16),
        scratch_shapes=scratch,
        compiler_params=_params(2, VMEM_LIMIT),
        name="attn_b",
    )(*args)


def _suffix_ones(t):
    r = lax.broadcasted_iota(jnp.int32, (t, t), 0)
    c = lax.broadcasted_iota(jnp.int32, (t, t), 1)
    return jnp.where(r >= c, 1.0, 0.0).astype(BF16)


def _stick_weights(z, blk, ones_ref, carry, before):
    neg_abs = pltpu.bitcast(pltpu.bitcast(z, jnp.uint32) | jnp.uint32(0x80000000), F32)
    soft = jnp.maximum(z, 0.0) + jnp.log(1.0 + jnp.exp(neg_abs))
    nblk = z.shape[1] // blk
    ws = [None] * nblk
    for j in reversed(range(nblk)):
        cols = slice(j * blk, (j + 1) * blk)
        masked = before is not None and j == nblk - 1
        keep = jnp.where(before, soft[:, cols], 0.0) if masked else soft[:, cols]
        incl = jnp.dot(keep.astype(BF16), ones_ref[...], preferred_element_type=F32)
        w = jnp.exp(z[:, cols] - (incl + carry))
        if masked:
            w = jnp.where(before, w, 0.0)
        ws[j] = w.astype(BF16)
        carry = carry + incl[:, 0:1]
    return (ws[0] if nblk == 1 else jnp.concatenate(ws, axis=-1)), carry


def _attn_c_kernel(*refs, tq, nq, plen, tpre):
    if plen:
        q_ref, k_ref, v_ref, kp_ref, vp_ref, o_ref, kt, vb, u_self, kpt, vpb, u_pre = refs
    else:
        q_ref, k_ref, v_ref, o_ref, kt, vb, u_self = refs
    lo = _lo_mask()
    _prep_kv(k_ref, v_ref, kt, vb)
    u_self[...] = _suffix_ones(tq)
    if plen:
        _prep_kv(kp_ref, vp_ref, kpt, vpb)
        u_pre[...] = _suffix_ones(tpre)
    _, before = _stacked_causal(tq)
    for i in range(nq):
        d0, d1 = i * tq, (i + 1) * tq
        q2 = _stack_heads(q_ref[0, d0:d1, :], lo)
        z = jnp.dot(q2, kt[:, 0:d1], preferred_element_type=F32)
        w, carry = _stick_weights(z, tq, u_self, jnp.zeros((2 * tq, 1), F32), before)
        o2 = jnp.dot(w, vb[0:d1, :], preferred_element_type=F32)
        if plen:
            wp, _ = _stick_weights(jnp.dot(q2, kpt[...], preferred_element_type=F32), tpre, u_pre, carry, None)
            o2 = o2 + jnp.dot(wp, vpb[...], preferred_element_type=F32)
        o_ref[0, d0:d1, :] = _unstack_heads(o2, lo).astype(o_ref.dtype)


def _attn_c(q, k, v, k_pre=None, v_pre=None):
    b, t, dc = q.shape
    pairs = dc // LANES
    tq = min(ATT_T, t)
    new = lambda: pl.BlockSpec((1, t, LANES), lambda bi, p: (bi, 0, p))
    in_specs = [new(), new(), new()]
    args = [q, k, v]
    scratch = [pltpu.VMEM((LANES, t), BF16), pltpu.VMEM((t, LANES), BF16), pltpu.VMEM((tq, tq), BF16)]
    plen = tpre = 0
    if k_pre is not None:
        plen = k_pre.shape[1]
        tpre = min(ATT_T, plen)
        old = lambda: pl.BlockSpec((1, plen, LANES), lambda bi, p: (bi, 0, p))
        in_specs += [old(), old()]
        args += [k_pre, v_pre]
        scratch += [pltpu.VMEM((LANES, plen), BF16), pltpu.VMEM((plen, LANES), BF16),
                    pltpu.VMEM((tpre, tpre), BF16)]
    return pl.pallas_call(
        functools.partial(_attn_c_kernel, tq=tq, nq=t // tq, plen=plen, tpre=tpre),
        grid=(b, pairs),
        in_specs=in_specs,
        out_specs=new(),
        out_shape=jax.ShapeDtypeStruct((b, t, dc), BF16),
        scratch_shapes=scratch,
        compiler_params=_params(2, VMEM_LIMIT),
        name="attn_c",
    )(*args)


def _softmax_rows(x):
    m = jnp.max(x, axis=-1, keepdims=True)
    p = jnp.exp(x - m)
    return p, jnp.sum(p, axis=-1, keepdims=True)


def _head_norm_x(y, gain):
    parts = []
    for h in range(N_HEADS_X):
        blk = y[:, h * HEAD_DIM_X:(h + 1) * HEAD_DIM_X]
        ms = jnp.mean(blk * blk, axis=-1, keepdims=True)
        parts.append(blk * lax.rsqrt(ms + EPS) * gain)
    return parts


def _mem_kv_kernel(m_ref, g_ref, w_ref, gain_ref, k_ref, v_ref, kb_ref, vb_ref):
    xn = _rms_rows(m_ref[...], g_ref[...]).astype(BF16)
    dx = w_ref.shape[1] // 2
    k = jnp.dot(xn, w_ref[:, :dx], preferred_element_type=F32)
    for h, blk in enumerate(_head_norm_x(k, gain_ref[...])):
        k_ref[:, h * HEAD_DIM_X:(h + 1) * HEAD_DIM_X] = blk
        kb_ref[:, h * HEAD_DIM_X:(h + 1) * HEAD_DIM_X] = blk.astype(BF16)
    v = jnp.dot(xn, w_ref[:, dx:], preferred_element_type=F32)
    v_ref[...] = v
    vb_ref[...] = v.astype(BF16)


def _mem_kv(mem2d, g, w_kv, k_gain):
    n = mem2d.shape[0]
    ts = _row_tile(n)
    dx = w_kv.shape[1] // 2
    row = lambda width: pl.BlockSpec((ts, width), lambda i: (i, 0))
    return pl.pallas_call(
        _mem_kv_kernel,
        grid=(n // ts,),
        in_specs=[row(D_MODEL), _const_spec(g.shape), _const_spec(w_kv.shape), _const_spec(k_gain.shape)],
        out_specs=[row(dx)] * 4,
        out_shape=(jax.ShapeDtypeStruct((n, dx), F32), jax.ShapeDtypeStruct((n, dx), F32),
                   jax.ShapeDtypeStruct((n, dx), BF16), jax.ShapeDtypeStruct((n, dx), BF16)),
        compiler_params=_params(1, VMEM_LIMIT),
        name="mem_kv",
    )(mem2d, g, w_kv, k_gain)


def _post_kernel(h_ref, o1_ref, o2_ref, wout_ref, g_ref, wq_ref, qgain_ref, mk_ref, mv_ref, wo_ref, y_ref):
    half = o1_ref.shape[2]
    h1 = (h_ref[0]
          + jnp.dot(o1_ref[0], wout_ref[:half, :], preferred_element_type=F32)
          + jnp.dot(o2_ref[0], wout_ref[half:, :], preferred_element_type=F32))
    hn = _rms_rows(h1, g_ref[...]).astype(BF16)
    qx = jnp.dot(hn, wq_ref[...], preferred_element_type=F32)
    heads = []
    for h, qh in enumerate(_head_norm_x(qx, qgain_ref[...])):
        cols = slice(h * HEAD_DIM_X, (h + 1) * HEAD_DIM_X)
        s = _dot_nt((qh * X_SCALE).astype(BF16), mk_ref[0, :, cols])
        p, l = _softmax_rows(s)
        oh = jnp.dot(p.astype(BF16), mv_ref[0, :, cols], preferred_element_type=F32) / l
        heads.append(oh.astype(BF16))
    y_ref[0] = h1 + jnp.dot(jnp.concatenate(heads, axis=-1), wo_ref[...], preferred_element_type=F32)


def _post(h, o1, o2, w_out, g_cross, w_q, q_gain, mk, mv, w_o):
    b, s, d = h.shape
    ts = _row_tile(s)
    half = w_out.shape[0] // 2
    n_mem = mk.shape[1]
    tile = lambda width: pl.BlockSpec((1, ts, width), lambda bi, i: (bi, i, 0))
    o1_spec = pl.BlockSpec((1, ts, half), lambda bi, i: (bi, i, 0))
    if o2 is None:
        o2 = o1
        o2_spec = pl.BlockSpec((1, ts, half), lambda bi, i: (bi, i, 1))
    else:
        o2_spec = o1_spec
    mem = lambda: pl.BlockSpec((1, n_mem, d), lambda bi, i: (bi, 0, 0))
    return pl.pallas_call(
        _post_kernel,
        grid=(b, s // ts),
        in_specs=[tile(d), o1_spec, o2_spec, _const_spec(w_out.shape), _const_spec(g_cross.shape),
                  _const_spec(w_q.shape), _const_spec(q_gain.shape), mem(), mem(), _const_spec(w_o.shape)],
        out_specs=tile(d),
        out_shape=jax.ShapeDtypeStruct((b, s, d), F32),
        compiler_params=_params(2, VMEM_LIMIT),
        name="post_mixer_cross",
    )(h, o1, o2, w_out, g_cross, w_q, q_gain, mk, mv, w_o)


def _mlp_kernel(h_ref, g_ref, wup_ref, wdn_ref, y_ref):
    h = h_ref[...]
    hn = _rms_rows(h, g_ref[...]).astype(BF16)
    acc = h
    for c in range(wup_ref.shape[1] // FF_CHUNK):
        cols = slice(c * FF_CHUNK, (c + 1) * FF_CHUNK)
        u = jnp.maximum(jnp.dot(hn, wup_ref[:, cols], preferred_element_type=F32), 0.0)
        acc = acc + jnp.dot((u * u).astype(BF16), wdn_ref[cols, :], preferred_element_type=F32)
    y_ref[...] = acc


def _mlp(h2d, g, w_up, w_down):
    n = h2d.shape[0]
    ts = _row_tile(n)
    row = pl.BlockSpec((ts, D_MODEL), lambda i: (i, 0))
    return pl.pallas_call(
        _mlp_kernel,
        grid=(n // ts,),
        in_specs=[row, _const_spec(g.shape), _const_spec(w_up.shape), _const_spec(w_down.shape)],
        out_specs=row,
        out_shape=jax.ShapeDtypeStruct((n, D_MODEL), F32),
        compiler_params=_params(1, VMEM_LIMIT),
        name="mlp",
    )(h2d, g, w_up, w_down)


def _row_vec(x):
    return x.reshape(1, -1).astype(F32)


def _ab_weights(w_in, b_f, qk_gain):
    dab3 = w_in.shape[1] - b_f.shape[0]
    n_f = b_f.shape[0]
    w_main = w_in[:, :dab3].astype(BF16)
    wf = jnp.zeros((w_in.shape[0], LANES), F32).at[:, :n_f].set(w_in[:, dab3:]).astype(BF16)
    bf = jnp.zeros((1, LANES), F32).at[0, :n_f].set(b_f)
    gains = jnp.concatenate([qk_gain, qk_gain], axis=-1).astype(F32)
    return w_main, wf, bf, gains


def _heads_first(x):
    return jnp.transpose(x, (0, 2, 1))


def kernel(x_prompt, x_sample, mem_prompt, cache_a_k, cache_a_v, cache_b_k, cache_b_v, cache_b_logf,
           cache_c_k, cache_c_v, cache_mem_k, cache_mem_v, norm_mix, norm_cross, norm_mlp, norm_mem,
           ab_w_in, ab_forget_bias, ab_qk_gain, ab_rel_bias, ab_w_out, sb_w_in, sb_w_out,
           x_w_q, x_w_kv, x_qk_gain, x_w_o, mlp_w_up, mlp_w_down):
    bp, sp, d = x_prompt.shape
    bs, ss, _ = x_sample.shape
    depth = norm_mix.shape[0]
    n_mem = mem_prompt.shape[1]
    hb = ab_forget_bias.shape[1]
    hp, hs = x_prompt, x_sample
    outs = {name: [] for name in ("a_kp", "a_vp", "a_ks", "a_vs", "b_kp", "b_vp", "b_fp", "b_ks", "b_vs",
                                  "b_fs", "c_kp", "c_vp", "c_ks", "c_vs", "m_kp", "m_vp")}
    mem2d = mem_prompt.reshape(bp * n_mem, d)

    for layer in range(depth):
        i = layer // 2
        g_mix = _row_vec(norm_mix[layer])
        if layer % 2 == 0:
            w_main, wf, bf, gains = _ab_weights(ab_w_in[i], ab_forget_bias[i], ab_qk_gain[i])
            da = w_main.shape[1] // 6
            ha = da // HEAD_DIM
            qa, ka, va, qb, kb, vb, lf = _proj_ab(hp.reshape(bp * sp, d), g_mix, w_main, wf, bf, gains)
            r3 = lambda x, b_, s_: x.reshape(b_, s_, x.shape[-1])
            qa, ka, va, qb, kb, vb = (r3(x, bp, sp) for x in (qa, ka, va, qb, kb, vb))
            logf = r3(lf, bp, sp)[..., :hb]
            c_p = _cumsum_rows(_heads_first(logf).reshape(bp * hb, sp)).reshape(bp, hb, sp)
            oa = _attn_a_prompt(qa, ka, va, ab_rel_bias[i])
            ob = _attn_b(qb, kb, vb, c_p)
            keep = min(A_PAST, sp)
            outs["a_kp"].append(ka[:, sp - keep:].reshape(bp, keep, ha, HEAD_DIM))
            outs["a_vp"].append(va[:, sp - keep:].reshape(bp, keep, ha, HEAD_DIM))
            outs["b_kp"].append(kb.reshape(bp, sp, hb, HEAD_DIM))
            outs["b_vp"].append(vb.reshape(bp, sp, hb, HEAD_DIM))
            outs["b_fp"].append(logf)
            mix_p = (oa, ob)
            qa, ka, va, qb, kb, vb, lf = _proj_ab(hs.reshape(bs * ss, d), g_mix, w_main, wf, bf, gains)
            qa, ka, va, qb, kb, vb = (r3(x, bs, ss) for x in (qa, ka, va, qb, kb, vb))
            logf = r3(lf, bs, ss)[..., :hb]
            past = cache_b_k.shape[2]
            n_band = cache_a_k.shape[2]
            lf_all = jnp.concatenate([_heads_first(cache_b_logf[i].astype(F32)), _heads_first(logf)], axis=-1)
            total = past + ss
            padded = -(-total // LANES) * LANES
            lf_all = jnp.pad(lf_all, ((0, 0), (0, 0), (0, padded - total)))
            c_all = _cumsum_rows(lf_all.reshape(bs * hb, padded)).reshape(bs, hb, padded)
            oa = _attn_a_sample(qa, ka, va, cache_a_k[i].reshape(bs, n_band, da),
                                cache_a_v[i].reshape(bs, n_band, da), ab_rel_bias[i], past)
            ob = _attn_b(qb, kb, vb, c_all[..., past:total],
                         cache_b_k[i].reshape(bs, past, da), cache_b_v[i].reshape(bs, past, da),
                         c_all[..., :past])
            outs["a_ks"].append(ka.reshape(bs, ss, ha, HEAD_DIM))
            outs["a_vs"].append(va.reshape(bs, ss, ha, HEAD_DIM))
            outs["b_ks"].append(kb.reshape(bs, ss, hb, HEAD_DIM))
            outs["b_vs"].append(vb.reshape(bs, ss, hb, HEAD_DIM))
            outs["b_fs"].append(logf)
            mix_s = (oa, ob)
            w_out = ab_w_out[i].astype(BF16)
        else:
            w_sb = sb_w_in[i].astype(BF16)
            dc = w_sb.shape[1] // 3
            hc = dc // HEAD_DIM
            q, k, v = _proj_sb(hp.reshape(bp * sp, d), g_mix, w_sb)
            q, k, v = (x.reshape(bp, sp, dc) for x in (q, k, v))
            mix_p = (_attn_c(q, k, v), None)
            outs["c_kp"].append(k.reshape(bp, sp, hc, HEAD_DIM))
            outs["c_vp"].append(v.reshape(bp, sp, hc, HEAD_DIM))
            q, k, v = _proj_sb(hs.reshape(bs * ss, d), g_mix, w_sb)
            q, k, v = (x.reshape(bs, ss, dc) for x in (q, k, v))
            past = cache_c_k.shape[2]
            mix_s = (_attn_c(q, k, v, cache_c_k[i].reshape(bs, past, dc), cache_c_v[i].reshape(bs, past, dc)),
                     None)
            outs["c_ks"].append(k.reshape(bs, ss, hc, HEAD_DIM))
            outs["c_vs"].append(v.reshape(bs, ss, hc, HEAD_DIM))
            w_out = sb_w_out[i].astype(BF16)

        mk, mv, mkb, mvb = _mem_kv(mem2d, _row_vec(norm_mem[layer]), x_w_kv[layer].astype(BF16),
                                   _row_vec(x_qk_gain[layer, 1]))
        outs["m_kp"].append(mk.reshape(bp, n_mem, N_HEADS_X, HEAD_DIM_X))
        outs["m_vp"].append(mv.reshape(bp, n_mem, N_HEADS_X, HEAD_DIM_X))
        g_cross = _row_vec(norm_cross[layer])
        w_q, w_o = x_w_q[layer].astype(BF16), x_w_o[layer].astype(BF16)
        q_gain = _row_vec(x_qk_gain[layer, 0])
        hp = _post(hp, mix_p[0], mix_p[1], w_out, g_cross, w_q, q_gain,
                   mkb.reshape(bp, n_mem, d), mvb.reshape(bp, n_mem, d), w_o)
        hs = _post(hs, mix_s[0], mix_s[1], w_out, g_cross, w_q, q_gain,
                   cache_mem_k[layer].reshape(bs, n_mem, d).astype(BF16),
                   cache_mem_v[layer].reshape(bs, n_mem, d).astype(BF16), w_o)
        g_mlp = _row_vec(norm_mlp[layer])
        w_up, w_dn = mlp_w_up[layer].astype(BF16), mlp_w_down[layer].astype(BF16)
        hp = _mlp(hp.reshape(bp * sp, d), g_mlp, w_up, w_dn).reshape(bp, sp, d)
        hs = _mlp(hs.reshape(bs * ss, d), g_mlp, w_up, w_dn).reshape(bs, ss, d)

    st = lambda name: jnp.stack(outs[name])
    return (hp, hs, st("a_kp"), st("a_vp"), st("a_ks"), st("a_vs"),
            st("b_kp"), st("b_vp"), st("b_fp"), st("b_ks"), st("b_vs"), st("b_fs"),
            st("c_kp"), st("c_vp"), st("c_ks"), st("c_vs"), st("m_kp"), st("m_vp"))
```

```python
import functools

import numpy as np
import jax
import jax.numpy as jnp
from jax import lax
from jax.experimental import pallas as pl
from jax.experimental.pallas import tpu as pltpu

F32 = jnp.float32
BF16 = jnp.bfloat16

D_MODEL = 1024
HEAD_DIM = 64
LANES = 128
CHUNK = 64
A_LEFT_CHUNKS = 8
A_PAST = A_LEFT_CHUNKS * CHUNK
REL_CLIP = 128
N_HEADS_X = 4
HEAD_DIM_X = D_MODEL // N_HEADS_X
D_FF = 4 * D_MODEL
EPS = 1e-6
NEG = -1e30
QK_SCALE = HEAD_DIM ** -0.5
X_SCALE = HEAD_DIM_X ** -0.5

A_TQ = 256
A_WIN = A_PAST + A_TQ
ATT_T = 256
FF_CHUNK = 1024
VMEM_LIMIT = 56 * 1024 * 1024


def _row_tile(n, cap=512):
    t = cap
    while n % t:
        t //= 2
    return t


def _params(n_axes, vmem=None):
    return pltpu.CompilerParams(dimension_semantics=("arbitrary",) * n_axes,
                                vmem_limit_bytes=vmem)


def _const_spec(shape):
    nd = len(shape)
    return pl.BlockSpec(shape, lambda *_: (0,) * nd, pipeline_mode=pl.Buffered(1))


def _rms_rows(x, g):
    ms = jnp.mean(x * x, axis=-1, keepdims=True)
    return x * lax.rsqrt(ms + EPS) * g


def _log_sigmoid(x):
    return jnp.minimum(x, 0.0) - jnp.log1p(jnp.exp(-jnp.abs(x)))


def _lo_mask():
    return lax.broadcasted_iota(jnp.int32, (1, LANES), 1) < HEAD_DIM


def _pair_headnorm(blk, gain, lo):
    sq = blk * blk
    s_lo = jnp.sum(jnp.where(lo, sq, 0.0), axis=-1, keepdims=True)
    s_hi = jnp.sum(jnp.where(lo, 0.0, sq), axis=-1, keepdims=True)
    inv = jnp.where(lo, lax.rsqrt(s_lo * (1.0 / HEAD_DIM) + EPS),
                    lax.rsqrt(s_hi * (1.0 / HEAD_DIM) + EPS))
    return blk * inv * gain


def _dot_nt(a, b):
    return lax.dot_general(a, b, (((1,), (1,)), ((), ())), preferred_element_type=F32)


def _store_rows(ref, cols, blk, time_minor):
    if time_minor:
        ref[0, cols, :] = jnp.transpose(blk)
    else:
        ref[:, cols] = blk.astype(ref.dtype)


def _proj_ab_kernel(h_ref, g_ref, w_ref, wf_ref, bf_ref, gain_ref,
                    qa_ref, ka_ref, va_ref, qb_ref, kb_ref, vb_ref, lf_ref, *, time_minor):
    h = h_ref[0] if time_minor else h_ref[...]
    xn = _rms_rows(h, g_ref[...]).astype(BF16)
    lo = _lo_mask()
    seg_w = w_ref.shape[1] // 6
    outs = ((qa_ref, 0, True), (ka_ref, 1, False), (va_ref, None, False),
            (qb_ref, 2, True), (kb_ref, 3, False), (vb_ref, None, False))
    for s, (ref, gi, is_q) in enumerate(outs):
        y = jnp.dot(xn, w_ref[:, s * seg_w:(s + 1) * seg_w], preferred_element_type=F32)
        for p in range(seg_w // LANES):
            cols = slice(p * LANES, (p + 1) * LANES)
            blk = y[:, cols]
            if gi is not None:
                blk = _pair_headnorm(blk, gain_ref[gi:gi + 1, :], lo)
            if is_q:
                q_dst = ref.at[0] if time_minor else ref
                q_dst[:, cols] = (blk * QK_SCALE).astype(ref.dtype)
            else:
                _store_rows(ref, cols, blk, time_minor)
    lf = _log_sigmoid(jnp.dot(xn, wf_ref[...], preferred_element_type=F32) + bf_ref[...])
    if time_minor:
        lf_ref[0] = jnp.transpose(lf)[:lf_ref.shape[1], :]
    else:
        lf_ref[...] = lf


def _proj_ab(h, g, w_main, wf, bf, gains, n_forget):
    time_minor = h.ndim == 3
    dab = w_main.shape[1] // 6
    consts = [_const_spec(g.shape), _const_spec(w_main.shape), _const_spec(wf.shape),
              _const_spec(bf.shape), _const_spec(gains.shape)]
    if time_minor:
        b, s, d = h.shape
        ts = _row_tile(s)
        grid = (b, s // ts)
        nat = lambda w: pl.BlockSpec((1, ts, w), lambda bi, i: (bi, i, 0))
        tm = lambda w: pl.BlockSpec((1, w, ts), lambda bi, i: (bi, 0, i))
        in_specs = [nat(d)] + consts
        out_specs = [nat(dab), tm(dab), tm(dab), nat(dab), tm(dab), tm(dab), tm(n_forget)]
        sds = lambda shape, dt: jax.ShapeDtypeStruct(shape, dt)
        out_shape = (sds((b, s, dab), BF16), sds((b, dab, s), F32), sds((b, dab, s), F32),
                     sds((b, s, dab), BF16), sds((b, dab, s), F32), sds((b, dab, s), F32),
                     sds((b, n_forget, s), F32))
    else:
        n = h.shape[0]
        ts = _row_tile(n)
        grid = (n // ts,)
        row = lambda w: pl.BlockSpec((ts, w), lambda i: (i, 0))
        in_specs = [row(D_MODEL)] + consts
        out_specs = [row(dab)] * 6 + [row(LANES)]
        out_shape = tuple(jax.ShapeDtypeStruct((n, dab), dt) for dt in (BF16, F32, F32, BF16, F32, F32)) + (
            jax.ShapeDtypeStruct((n, LANES), F32),)
    return pl.pallas_call(
        functools.partial(_proj_ab_kernel, time_minor=time_minor),
        grid=grid,
        in_specs=in_specs,
        out_specs=out_specs,
        out_shape=out_shape,
        compiler_params=_params(len(grid), VMEM_LIMIT),
        name="proj_ab",
    )(h, g, w_main, wf, bf, gains)


def _proj_sb_kernel(h_ref, g_ref, w_ref, q_ref, k_ref, v_ref, *, time_minor):
    h = h_ref[0] if time_minor else h_ref[...]
    xn = _rms_rows(h, g_ref[...]).astype(BF16)
    dc = w_ref.shape[1] // 3
    q = jnp.dot(xn, w_ref[:, :dc], preferred_element_type=F32)
    q_dst = q_ref.at[0] if time_minor else q_ref
    q_dst[...] = (q * QK_SCALE).astype(BF16)
    for ref, c0 in ((k_ref, dc), (v_ref, 2 * dc)):
        y = jnp.dot(xn, w_ref[:, c0:c0 + dc], preferred_element_type=F32)
        for p in range(dc // LANES):
            cols = slice(p * LANES, (p + 1) * LANES)
            _store_rows(ref, cols, y[:, cols], time_minor)


def _proj_sb(h, g, w):
    time_minor = h.ndim == 3
    dc = w.shape[1] // 3
    consts = [_const_spec(g.shape), _const_spec(w.shape)]
    if time_minor:
        b, s, d = h.shape
        ts = _row_tile(s)
        grid = (b, s // ts)
        nat = lambda width: pl.BlockSpec((1, ts, width), lambda bi, i: (bi, i, 0))
        tm = pl.BlockSpec((1, dc, ts), lambda bi, i: (bi, 0, i))
        in_specs = [nat(d)] + consts
        out_specs = [nat(dc), tm, tm]
        out_shape = (jax.ShapeDtypeStruct((b, s, dc), BF16), jax.ShapeDtypeStruct((b, dc, s), F32),
                     jax.ShapeDtypeStruct((b, dc, s), F32))
    else:
        n = h.shape[0]
        ts = _row_tile(n)
        grid = (n // ts,)
        row = lambda width: pl.BlockSpec((ts, width), lambda i: (i, 0))
        in_specs = [row(D_MODEL)] + consts
        out_specs = [row(dc)] * 3
        out_shape = (jax.ShapeDtypeStruct((n, dc), BF16), jax.ShapeDtypeStruct((n, dc), F32),
                     jax.ShapeDtypeStruct((n, dc), F32))
    return pl.pallas_call(
        functools.partial(_proj_sb_kernel, time_minor=time_minor),
        grid=grid,
        in_specs=in_specs,
        out_specs=out_specs,
        out_shape=out_shape,
        compiler_params=_params(len(grid), VMEM_LIMIT),
        name="proj_sb",
    )(h, g, w)


def _split3(x):
    hi = x.astype(BF16)
    r = x - hi.astype(F32)
    mid = r.astype(BF16)
    lo = (r - mid.astype(F32)).astype(BF16)
    return hi, mid, lo


def _cumsum_kernel(x_ref, o_ref):
    rows, length = x_ref.shape
    r_i = lax.broadcasted_iota(jnp.int32, (LANES, LANES), 0)
    c_i = lax.broadcasted_iota(jnp.int32, (LANES, LANES), 1)
    tri = jnp.where(r_i <= c_i, 1.0, 0.0).astype(BF16)
    carry = jnp.zeros((rows, 1), F32)
    for b in range(length // LANES):
        hi, mid, lo = _split3(x_ref[:, b * LANES:(b + 1) * LANES])
        c = (jnp.dot(hi, tri, preferred_element_type=F32)
             + jnp.dot(mid, tri, preferred_element_type=F32)
             + jnp.dot(lo, tri, preferred_element_type=F32)) + carry
        o_ref[:, b * LANES:(b + 1) * LANES] = c
        carry = c[:, LANES - 1:LANES]


def _cumsum_rows(x):
    return pl.pallas_call(
        _cumsum_kernel,
        out_shape=jax.ShapeDtypeStruct(x.shape, F32),
        name="cumsum_logf",
    )(x)


def _prep_kv(k_ref, v_ref, kt, vb, time_minor):
    if time_minor:
        kt[...] = k_ref[0].astype(BF16)
        vb[...] = jnp.transpose(v_ref[0]).astype(BF16)
    else:
        kt[...] = jnp.transpose(k_ref[0]).astype(BF16)
        vb[...] = v_ref[0].astype(BF16)


def _stack_heads(q, lo):
    qf = q.astype(F32)
    return jnp.concatenate([jnp.where(lo, qf, 0.0), jnp.where(lo, 0.0, qf)], axis=0).astype(BF16)


def _unstack_heads(o2, lo):
    t = o2.shape[0] // 2
    return jnp.where(lo, o2[:t], o2[t:])


def _softmax_pv(pieces):
    m = None
    for x, _ in pieces:
        mx = jnp.max(x, axis=-1, keepdims=True)
        m = mx if m is None else jnp.maximum(m, mx)
    l, o = None, None
    for x, vv in pieces:
        p = jnp.exp(x - m)
        ls = jnp.sum(p, axis=-1, keepdims=True)
        os_ = jnp.dot(p.astype(BF16), vv, preferred_element_type=F32)
        l = ls if l is None else l + ls
        o = os_ if o is None else o + os_
    return o / l


def _stacked_causal(t):
    r = lax.broadcasted_iota(jnp.int32, (2 * t, t), 0)
    r = jnp.where(r >= t, r - t, r)
    c = lax.broadcasted_iota(jnp.int32, (2 * t, t), 1)
    return c <= r, c < r


def _attn_a_prompt_kernel(q_ref, k_ref, v_ref, bias_ref, o_ref, kt, vb, *, nq, win, ncase):
    lo = _lo_mask()
    _prep_kv(k_ref, v_ref, kt, vb, True)
    for i in range(nq):
        q0 = i * A_TQ
        case = min(i, ncase - 1)
        w0 = max(q0 + A_TQ - win, 0)
        wlen = min(win, q0 + A_TQ)
        q2 = _stack_heads(q_ref[0, q0:q0 + A_TQ, :], lo)
        x = jnp.dot(q2, kt[:, w0:w0 + wlen], preferred_element_type=F32) + bias_ref[0, case, :, :wlen]
        o2 = _softmax_pv([(x, vb[w0:w0 + wlen, :])])
        o_ref[0, q0:q0 + A_TQ, :] = _unstack_heads(o2, lo).astype(o_ref.dtype)


def _toeplitz_bias(rel_bias, rows, cols, shift):
    p = rows + cols - 1
    idx = np.clip(np.arange(p) - (rows - 1) + shift, -REL_CLIP, REL_CLIP) + REL_CLIP
    e = jnp.transpose(rel_bias[idx]).astype(F32)
    h = e.shape[0]
    skew = jnp.tile(e, (1, rows + 1))[:, :rows * (p + 1)].reshape(h, rows, p + 1)
    return skew[:, ::-1, :cols]


def _band_bias_prompt(rel_bias, s_len):
    win = min(A_WIN, s_len)
    nq = s_len // A_TQ
    ncase = min(nq, A_PAST // A_TQ + 1)
    r = np.arange(A_TQ)[:, None]
    j = np.arange(win)[None, :]
    tabs = []
    for c in range(ncase):
        q_off = c * A_TQ if (c + 1) * A_TQ <= win else win - A_TQ
        qc, kc = (q_off + r) // CHUNK, j // CHUNK
        ok = (kc <= qc) & (kc >= qc - A_LEFT_CHUNKS)
        tabs.append(jnp.where(ok[None], _toeplitz_bias(rel_bias, A_TQ, win, -q_off), NEG))
    tab = jnp.stack(tabs, axis=1)
    h = tab.shape[0]
    tab = jnp.transpose(tab.reshape(h // 2, 2, ncase, A_TQ, win), (0, 2, 1, 3, 4))
    return tab.reshape(h // 2, ncase, 2 * A_TQ, win), win, ncase


def _attn_a_prompt(q, k_t, v_t, rel_bias):
    b, s, da = q.shape
    pairs = da // LANES
    bias, win, ncase = _band_bias_prompt(rel_bias, s)
    nq = s // A_TQ
    blk = lambda: pl.BlockSpec((1, s, LANES), lambda bi, p: (bi, 0, p))
    blk_t = lambda: pl.BlockSpec((1, LANES, s), lambda bi, p: (bi, p, 0))
    return pl.pallas_call(
        functools.partial(_attn_a_prompt_kernel, nq=nq, win=win, ncase=ncase),
        grid=(b, pairs),
        in_specs=[blk(), blk_t(), blk_t(),
                  pl.BlockSpec((1, ncase, 2 * A_TQ, win), lambda bi, p: (p, 0, 0, 0))],
        out_specs=blk(),
        out_shape=jax.ShapeDtypeStruct((b, s, da), BF16),
        scratch_shapes=[pltpu.VMEM((LANES, s), BF16), pltpu.VMEM((s, LANES), BF16)],
        compiler_params=_params(2, VMEM_LIMIT),
        name="attn_a_prompt",
    )(q, k_t, v_t, bias)


def _attn_a_sample_kernel(q_ref, kc_ref, vc_ref, kn_ref, vn_ref, bc_ref, bn_ref, o_ref):
    lo = _lo_mask()
    q2 = _stack_heads(q_ref[0], lo)
    xc = jnp.dot(q2, kc_ref[0].astype(BF16), preferred_element_type=F32) + bc_ref[0]
    xn = _dot_nt(q2, kn_ref[0].astype(BF16)) + bn_ref[0]
    o2 = _softmax_pv([(xc, jnp.transpose(vc_ref[0]).astype(BF16)), (xn, vn_ref[0].astype(BF16))])
    o_ref[0] = _unstack_heads(o2, lo).astype(o_ref.dtype)


def _attn_a_sample(q, k_new, v_new, k_cache, v_cache, rel_bias, past):
    b, t, da = q.shape
    pairs = da // LANES
    n_band = k_cache.shape[2]
    q_pos = past + np.arange(t)[:, None]
    k_pos = past - n_band + np.arange(n_band + t)[None, :]
    qc, kc = q_pos // CHUNK, k_pos // CHUNK
    ok = (k_pos >= 0) & (kc <= qc) & (kc >= qc - A_LEFT_CHUNKS)
    tab = jnp.where(ok[None], _toeplitz_bias(rel_bias, t, n_band + t, -n_band), NEG)
    tab = tab.reshape(pairs, 2 * t, n_band + t)
    bias_c, bias_n = tab[..., :n_band], tab[..., n_band:]
    new = lambda: pl.BlockSpec((1, t, LANES), lambda bi, p: (bi, 0, p))
    old = lambda: pl.BlockSpec((1, LANES, n_band), lambda bi, p: (bi, p, 0))
    return pl.pallas_call(
        _attn_a_sample_kernel,
        grid=(b, pairs),
        in_specs=[new(), old(), old(), new(), new(),
                  pl.BlockSpec((1, 2 * t, n_band), lambda bi, p: (p, 0, 0)),
                  pl.BlockSpec((1, 2 * t, t), lambda bi, p: (p, 0, 0))],
        out_specs=new(),
        out_shape=jax.ShapeDtypeStruct((b, t, da), BF16),
        compiler_params=_params(2, VMEM_LIMIT),
        name="attn_a_sample",
    )(q, k_cache, v_cache, k_new, v_new, bias_c, bias_n)


def _attn_b_kernel(*refs, tq, nq, plen, time_minor):
    if plen:
        (q_ref, k_ref, v_ref, crow_ref, ccol_ref, kp_ref, vp_ref, cpre_ref, o_ref, kt, vb, kpt, vpb) = refs
    else:
        q_ref, k_ref, v_ref, crow_ref, ccol_ref, o_ref, kt, vb = refs
    pair = pl.program_id(1)
    lo = _lo_mask()
    _prep_kv(k_ref, v_ref, kt, vb, time_minor)
    if plen:
        _prep_kv(kp_ref, vp_ref, kpt, vpb, True)
    keep, _ = _stacked_causal(tq)
    head_lane = lax.broadcasted_iota(jnp.int32, (1, ccol_ref.shape[2]), 1)

    def logits(q2, kt_cols, cq, ck_ref, c0, c1):
        z = jnp.dot(q2, kt_cols, preferred_element_type=F32)
        return jnp.concatenate([z[:tq] + (cq[0] - ck_ref[0, 0, 0:1, c0:c1]),
                                z[tq:] + (cq[1] - ck_ref[0, 0, 1:2, c0:c1])], axis=0)

    for i in range(nq):
        d0, d1 = i * tq, (i + 1) * tq
        q2 = _stack_heads(q_ref[0, d0:d1, :], lo)
        cq_all = ccol_ref[0, d0:d1, :]
        cq = [jnp.sum(jnp.where(head_lane == 2 * pair + e, cq_all, 0.0), axis=-1, keepdims=True)
              for e in range(2)]
        pieces = []
        if plen:
            pieces.append((logits(q2, kpt[...], cq, cpre_ref, 0, plen), vpb[...]))
        if i:
            pieces.append((logits(q2, kt[:, 0:d0], cq, crow_ref, 0, d0), vb[0:d0, :]))
        xd = logits(q2, kt[:, d0:d1], cq, crow_ref, d0, d1)
        pieces.append((jnp.where(keep, xd, NEG), vb[d0:d1, :]))
        o_ref[0, d0:d1, :] = _unstack_heads(_softmax_pv(pieces), lo).astype(o_ref.dtype)


def _attn_b(q, k, v, c_self, k_pre=None, v_pre=None, c_pre=None, *, time_minor):
    b, t, db = q.shape
    pairs = db // LANES
    n_heads = c_self.shape[1]
    tq = min(ATT_T, t)
    c_col = jnp.transpose(c_self, (0, 2, 1))
    new = lambda: pl.BlockSpec((1, t, LANES), lambda bi, p: (bi, 0, p))
    tm = lambda n: pl.BlockSpec((1, LANES, n), lambda bi, p: (bi, p, 0))
    kv = (lambda: tm(t)) if time_minor else new
    in_specs = [new(), kv(), kv(),
                pl.BlockSpec((1, 1, 2, t), lambda bi, p: (bi, p, 0, 0)),
                pl.BlockSpec((1, t, n_heads), lambda bi, p: (bi, 0, 0))]
    args = [q, k, v, c_self.reshape(b, pairs, 2, t), c_col]
    scratch = [pltpu.VMEM((LANES, t), BF16), pltpu.VMEM((t, LANES), BF16)]
    plen = 0
    if k_pre is not None:
        plen = k_pre.shape[2]
        in_specs += [tm(plen), tm(plen), pl.BlockSpec((1, 1, 2, plen), lambda bi, p: (bi, p, 0, 0))]
        args += [k_pre, v_pre, c_pre.reshape(b, pairs, 2, plen)]
        scratch += [pltpu.VMEM((LANES, plen), BF16), pltpu.VMEM((plen, LANES), BF16)]
    return pl.pallas_call(
        functools.partial(_attn_b_kernel, tq=tq, nq=t // tq, plen=plen, time_minor=time_minor),
        grid=(b, pairs),
        in_specs=in_specs,
        out_specs=new(),
        out_shape=jax.ShapeDtypeStruct((b, t, db), BF16),
        scratch_shapes=scratch,
        compiler_params=_params(2, VMEM_LIMIT),
        name="attn_b",
    )(*args)


def _suffix_ones(t):
    r = lax.broadcasted_iota(jnp.int32, (t, t), 0)
    c = lax.broadcasted_iota(jnp.int32, (t, t), 1)
    return jnp.where(r >= c, 1.0, 0.0).astype(BF16)


def _stick_weights(z, blk, ones_ref, carry, before):
    neg_abs = pltpu.bitcast(pltpu.bitcast(z, jnp.uint32) | jnp.uint32(0x80000000), F32)
    soft = jnp.maximum(z, 0.0) + jnp.log(1.0 + jnp.exp(neg_abs))
    nblk = z.shape[1] // blk
    ws = [None] * nblk
    for j in reversed(range(nblk)):
        cols = slice(j * blk, (j + 1) * blk)
        masked = before is not None and j == nblk - 1
        keep = jnp.where(before, soft[:, cols], 0.0) if masked else soft[:, cols]
        incl = jnp.dot(keep.astype(BF16), ones_ref[...], preferred_element_type=F32)
        w = jnp.exp(z[:, cols] - (incl + carry))
        if masked:
            w = jnp.where(before, w, 0.0)
        ws[j] = w.astype(BF16)
        carry = carry + incl[:, 0:1]
    return (ws[0] if nblk == 1 else jnp.concatenate(ws, axis=-1)), carry


def _attn_c_kernel(*refs, tq, nq, plen, tpre, time_minor):
    if plen:
        q_ref, k_ref, v_ref, kp_ref, vp_ref, o_ref, kt, vb, u_self, kpt, vpb, u_pre = refs
    else:
        q_ref, k_ref, v_ref, o_ref, kt, vb, u_self = refs
    lo = _lo_mask()
    _prep_kv(k_ref, v_ref, kt, vb, time_minor)
    u_self[...] = _suffix_ones(tq)
    if plen:
        _prep_kv(kp_ref, vp_ref, kpt, vpb, True)
        u_pre[...] = _suffix_ones(tpre)
    _, before = _stacked_causal(tq)
    for i in range(nq):
        d0, d1 = i * tq, (i + 1) * tq
        q2 = _stack_heads(q_ref[0, d0:d1, :], lo)
        z = jnp.dot(q2, kt[:, 0:d1], preferred_element_type=F32)
        w, carry = _stick_weights(z, tq, u_self, jnp.zeros((2 * tq, 1), F32), before)
        o2 = jnp.dot(w, vb[0:d1, :], preferred_element_type=F32)
        if plen:
            wp, _ = _stick_weights(jnp.dot(q2, kpt[...], preferred_element_type=F32), tpre, u_pre, carry, None)
            o2 = o2 + jnp.dot(wp, vpb[...], preferred_element_type=F32)
        o_ref[0, d0:d1, :] = _unstack_heads(o2, lo).astype(o_ref.dtype)


def _attn_c(q, k, v, k_pre=None, v_pre=None, *, time_minor):
    b, t, dc = q.shape
    pairs = dc // LANES
    tq = min(ATT_T, t)
    new = lambda: pl.BlockSpec((1, t, LANES), lambda bi, p: (bi, 0, p))
    tm = lambda n: pl.BlockSpec((1, LANES, n), lambda bi, p: (bi, p, 0))
    kv = (lambda: tm(t)) if time_minor else new
    in_specs = [new(), kv(), kv()]
    args = [q, k, v]
    scratch = [pltpu.VMEM((LANES, t), BF16), pltpu.VMEM((t, LANES), BF16), pltpu.VMEM((tq, tq), BF16)]
    plen = tpre = 0
    if k_pre is not None:
        plen = k_pre.shape[2]
        tpre = min(ATT_T, plen)
        in_specs += [tm(plen), tm(plen)]
        args += [k_pre, v_pre]
        scratch += [pltpu.VMEM((LANES, plen), BF16), pltpu.VMEM((plen, LANES), BF16),
                    pltpu.VMEM((tpre, tpre), BF16)]
    return pl.pallas_call(
        functools.partial(_attn_c_kernel, tq=tq, nq=t // tq, plen=plen, tpre=tpre, time_minor=time_minor),
        grid=(b, pairs),
        in_specs=in_specs,
        out_specs=new(),
        out_shape=jax.ShapeDtypeStruct((b, t, dc), BF16),
        scratch_shapes=scratch,
        compiler_params=_params(2, VMEM_LIMIT),
        name="attn_c",
    )(*args)


def _softmax_rows(x):
    m = jnp.max(x, axis=-1, keepdims=True)
    p = jnp.exp(x - m)
    return p, jnp.sum(p, axis=-1, keepdims=True)


def _head_norm_x(y, gain):
    parts = []
    for h in range(N_HEADS_X):
        blk = y[:, h * HEAD_DIM_X:(h + 1) * HEAD_DIM_X]
        ms = jnp.mean(blk * blk, axis=-1, keepdims=True)
        parts.append(blk * lax.rsqrt(ms + EPS) * gain)
    return parts


def _mem_kv_kernel(m_ref, g_ref, w_ref, gain_ref, k_ref, v_ref, kb_ref, vb_ref):
    xn = _rms_rows(m_ref[...], g_ref[...]).astype(BF16)
    dx = w_ref.shape[1] // 2
    k = jnp.dot(xn, w_ref[:, :dx], preferred_element_type=F32)
    for h, blk in enumerate(_head_norm_x(k, gain_ref[...])):
        k_ref[:, h * HEAD_DIM_X:(h + 1) * HEAD_DIM_X] = blk
        kb_ref[:, h * HEAD_DIM_X:(h + 1) * HEAD_DIM_X] = blk.astype(BF16)
    v = jnp.dot(xn, w_ref[:, dx:], preferred_element_type=F32)
    v_ref[...] = v
    vb_ref[...] = v.astype(BF16)


def _mem_kv(mem2d, g, w_kv, k_gain):
    n = mem2d.shape[0]
    ts = _row_tile(n)
    dx = w_kv.shape[1] // 2
    row = lambda width: pl.BlockSpec((ts, width), lambda i: (i, 0))
    return pl.pallas_call(
        _mem_kv_kernel,
        grid=(n // ts,),
        in_specs=[row(D_MODEL), _const_spec(g.shape), _const_spec(w_kv.shape), _const_spec(k_gain.shape)],
        out_specs=[row(dx)] * 4,
        out_shape=(jax.ShapeDtypeStruct((n, dx), F32), jax.ShapeDtypeStruct((n, dx), F32),
                   jax.ShapeDtypeStruct((n, dx), BF16), jax.ShapeDtypeStruct((n, dx), BF16)),
        compiler_params=_params(1, VMEM_LIMIT),
        name="mem_kv",
    )(mem2d, g, w_kv, k_gain)


def _post_kernel(h_ref, o1_ref, o2_ref, wout_ref, g_ref, wq_ref, qgain_ref, mk_ref, mv_ref, wo_ref, y_ref):
    half = o1_ref.shape[2]
    h1 = (h_ref[0]
          + jnp.dot(o1_ref[0], wout_ref[:half, :], preferred_element_type=F32)
          + jnp.dot(o2_ref[0], wout_ref[half:, :], preferred_element_type=F32))
    hn = _rms_rows(h1, g_ref[...]).astype(BF16)
    qx = jnp.dot(hn, wq_ref[...], preferred_element_type=F32)
    heads = []
    for h, qh in enumerate(_head_norm_x(qx, qgain_ref[...])):
        cols = slice(h * HEAD_DIM_X, (h + 1) * HEAD_DIM_X)
        s = _dot_nt((qh * X_SCALE).astype(BF16), mk_ref[0, :, cols])
        p, l = _softmax_rows(s)
        oh = jnp.dot(p.astype(BF16), mv_ref[0, :, cols], preferred_element_type=F32) / l
        heads.append(oh.astype(BF16))
    y_ref[0] = h1 + jnp.dot(jnp.concatenate(heads, axis=-1), wo_ref[...], preferred_element_type=F32)


def _post(h, o1, o2, w_out, g_cross, w_q, q_gain, mk, mv, w_o):
    b, s, d = h.shape
    ts = _row_tile(s)
    half = w_out.shape[0] // 2
    n_mem = mk.shape[1]
    tile = lambda width: pl.BlockSpec((1, ts, width), lambda bi, i: (bi, i, 0))
    o1_spec = pl.BlockSpec((1, ts, half), lambda bi, i: (bi, i, 0))
    if o2 is None:
        o2 = o1
        o2_spec = pl.BlockSpec((1, ts, half), lambda bi, i: (bi, i, 1))
    else:
        o2_spec = o1_spec
    mem = lambda: pl.BlockSpec((1, n_mem, d), lambda bi, i: (bi, 0, 0))
    return pl.pallas_call(
        _post_kernel,
        grid=(b, s // ts),
        in_specs=[tile(d), o1_spec, o2_spec, _const_spec(w_out.shape), _const_spec(g_cross.shape),
                  _const_spec(w_q.shape), _const_spec(q_gain.shape), mem(), mem(), _const_spec(w_o.shape)],
        out_specs=tile(d),
        out_shape=jax.ShapeDtypeStruct((b, s, d), F32),
        compiler_params=_params(2, VMEM_LIMIT),
        name="post_mixer_cross",
    )(h, o1, o2, w_out, g_cross, w_q, q_gain, mk, mv, w_o)


def _mlp_kernel(h_ref, g_ref, wup_ref, wdn_ref, y_ref):
    h = h_ref[...]
    hn = _rms_rows(h, g_ref[...]).astype(BF16)
    acc = h
    for c in range(wup_ref.shape[1] // FF_CHUNK):
        cols = slice(c * FF_CHUNK, (c + 1) * FF_CHUNK)
        u = jnp.maximum(jnp.dot(hn, wup_ref[:, cols], preferred_element_type=F32), 0.0)
        acc = acc + jnp.dot((u * u).astype(BF16), wdn_ref[cols, :], preferred_element_type=F32)
    y_ref[...] = acc


def _mlp(h2d, g, w_up, w_down):
    n = h2d.shape[0]
    ts = _row_tile(n)
    row = pl.BlockSpec((ts, D_MODEL), lambda i: (i, 0))
    return pl.pallas_call(
        _mlp_kernel,
        grid=(n // ts,),
        in_specs=[row, _const_spec(g.shape), _const_spec(w_up.shape), _const_spec(w_down.shape)],
        out_specs=row,
        out_shape=jax.ShapeDtypeStruct((n, D_MODEL), F32),
        compiler_params=_params(1, VMEM_LIMIT),
        name="mlp",
    )(h2d, g, w_up, w_down)


def _row_vec(x):
    return x.reshape(1, -1).astype(F32)


def _ab_weights(w_in, b_f, qk_gain):
    dab3 = w_in.shape[1] - b_f.shape[0]
    n_f = b_f.shape[0]
    w_main = w_in[:, :dab3].astype(BF16)
    wf = jnp.zeros((w_in.shape[0], LANES), F32).at[:, :n_f].set(w_in[:, dab3:]).astype(BF16)
    bf = jnp.zeros((1, LANES), F32).at[0, :n_f].set(b_f)
    gains = jnp.concatenate([qk_gain, qk_gain], axis=-1).astype(F32)
    return w_main, wf, bf, gains


def _time_minor(x):
    b, t = x.shape[:2]
    return jnp.transpose(x.reshape(b, t, -1), (0, 2, 1))


def _heads_out(x_t, n_heads):
    b, _, t = x_t.shape
    return jnp.transpose(x_t, (0, 2, 1)).reshape(b, t, n_heads, HEAD_DIM)


def kernel(x_prompt, x_sample, mem_prompt, cache_a_k, cache_a_v, cache_b_k, cache_b_v, cache_b_logf,
           cache_c_k, cache_c_v, cache_mem_k, cache_mem_v, norm_mix, norm_cross, norm_mlp, norm_mem,
           ab_w_in, ab_forget_bias, ab_qk_gain, ab_rel_bias, ab_w_out, sb_w_in, sb_w_out,
           x_w_q, x_w_kv, x_qk_gain, x_w_o, mlp_w_up, mlp_w_down):
    bp, sp, d = x_prompt.shape
    bs, ss, _ = x_sample.shape
    depth = norm_mix.shape[0]
    n_mem = mem_prompt.shape[1]
    hb = ab_forget_bias.shape[1]
    hp, hs = x_prompt, x_sample
    outs = {name: [] for name in ("a_kp", "a_vp", "a_ks", "a_vs", "b_kp", "b_vp", "b_fp", "b_ks", "b_vs",
                                  "b_fs", "c_kp", "c_vp", "c_ks", "c_vs", "m_kp", "m_vp")}
    mem2d = mem_prompt.reshape(bp * n_mem, d)

    for layer in range(depth):
        i = layer // 2
        g_mix = _row_vec(norm_mix[layer])
        if layer % 2 == 0:
            w_main, wf, bf, gains = _ab_weights(ab_w_in[i], ab_forget_bias[i], ab_qk_gain[i])
            da = w_main.shape[1] // 6
            ha = da // HEAD_DIM
            qa, ka, va, qb, kb, vb, lf = _proj_ab(hp, g_mix, w_main, wf, bf, gains, hb)
            c_p = _cumsum_rows(lf.reshape(bp * hb, sp)).reshape(bp, hb, sp)
            oa = _attn_a_prompt(qa, ka, va, ab_rel_bias[i])
            ob = _attn_b(qb, kb, vb, c_p, time_minor=True)
            keep = min(A_PAST, sp)
            outs["a_kp"].append(_heads_out(ka[:, :, sp - keep:], ha))
            outs["a_vp"].append(_heads_out(va[:, :, sp - keep:], ha))
            outs["b_kp"].append(_heads_out(kb, hb))
            outs["b_vp"].append(_heads_out(vb, hb))
            outs["b_fp"].append(jnp.transpose(lf, (0, 2, 1)))
            mix_p = (oa, ob)
            qa, ka, va, qb, kb, vb, lf = _proj_ab(hs.reshape(bs * ss, d), g_mix, w_main, wf, bf, gains, hb)
            r3 = lambda x: x.reshape(bs, ss, x.shape[-1])
            qa, ka, va, qb, kb, vb = (r3(x) for x in (qa, ka, va, qb, kb, vb))
            logf = r3(lf)[..., :hb]
            past = cache_b_k.shape[2]
            heads_first = lambda x: jnp.transpose(x, (0, 2, 1))
            lf_all = jnp.concatenate([heads_first(cache_b_logf[i].astype(F32)), heads_first(logf)], axis=-1)
            total = past + ss
            padded = -(-total // LANES) * LANES
            lf_all = jnp.pad(lf_all, ((0, 0), (0, 0), (0, padded - total)))
            c_all = _cumsum_rows(lf_all.reshape(bs * hb, padded)).reshape(bs, hb, padded)
            oa = _attn_a_sample(qa, ka, va, _time_minor(cache_a_k[i]), _time_minor(cache_a_v[i]),
                                ab_rel_bias[i], past)
            ob = _attn_b(qb, kb, vb, c_all[..., past:total], _time_minor(cache_b_k[i]), _time_minor(cache_b_v[i]),
                         c_all[..., :past], time_minor=False)
            outs["a_ks"].append(ka.reshape(bs, ss, ha, HEAD_DIM))
            outs["a_vs"].append(va.reshape(bs, ss, ha, HEAD_DIM))
            outs["b_ks"].append(kb.reshape(bs, ss, hb, HEAD_DIM))
            outs["b_vs"].append(vb.reshape(bs, ss, hb, HEAD_DIM))
            outs["b_fs"].append(logf)
            mix_s = (oa, ob)
            w_out = ab_w_out[i].astype(BF16)
        else:
            w_sb = sb_w_in[i].astype(BF16)
            dc = w_sb.shape[1] // 3
            hc = dc // HEAD_DIM
            q, k, v = _proj_sb(hp, g_mix, w_sb)
            mix_p = (_attn_c(q, k, v, time_minor=True), None)
            outs["c_kp"].append(_heads_out(k, hc))
            outs["c_vp"].append(_heads_out(v, hc))
            q, k, v = _proj_sb(hs.reshape(bs * ss, d), g_mix, w_sb)
            q, k, v = (x.reshape(bs, ss, dc) for x in (q, k, v))
            mix_s = (_attn_c(q, k, v, _time_minor(cache_c_k[i]), _time_minor(cache_c_v[i]), time_minor=False),
                     None)
            outs["c_ks"].append(k.reshape(bs, ss, hc, HEAD_DIM))
            outs["c_vs"].append(v.reshape(bs, ss, hc, HEAD_DIM))
            w_out = sb_w_out[i].astype(BF16)

        mk, mv, mkb, mvb = _mem_kv(mem2d, _row_vec(norm_mem[layer]), x_w_kv[layer].astype(BF16),
                                   _row_vec(x_qk_gain[layer, 1]))
        outs["m_kp"].append(mk.reshape(bp, n_mem, N_HEADS_X, HEAD_DIM_X))
        outs["m_vp"].append(mv.reshape(bp, n_mem, N_HEADS_X, HEAD_DIM_X))
        g_cross = _row_vec(norm_cross[layer])
        w_q, w_o = x_w_q[layer].astype(BF16), x_w_o[layer].astype(BF16)
        q_gain = _row_vec(x_qk_gain[layer, 0])
        hp = _post(hp, mix_p[0], mix_p[1], w_out, g_cross, w_q, q_gain,
                   mkb.reshape(bp, n_mem, d), mvb.reshape(bp, n_mem, d), w_o)
        hs = _post(hs, mix_s[0], mix_s[1], w_out, g_cross, w_q, q_gain,
                   cache_mem_k[layer].reshape(bs, n_mem, d).astype(BF16),
                   cache_mem_v[layer].reshape(bs, n_mem, d).astype(BF16), w_o)
        g_mlp = _row_vec(norm_mlp[layer])
        w_up, w_dn = mlp_w_up[layer].astype(BF16), mlp_w_down[layer].astype(BF16)
        hp = _mlp(hp.reshape(bp * sp, d), g_mlp, w_up, w_dn).reshape(bp, sp, d)
        hs = _mlp(hs.reshape(bs * ss, d), g_mlp, w_up, w_dn).reshape(bs, ss, d)

    st = lambda name: jnp.stack(outs[name])
    return (hp, hs, st("a_kp"), st("a_vp"), st("a_ks"), st("a_vs"),
            st("b_kp"), st("b_vp"), st("b_fp"), st("b_ks"), st("b_vs"), st("b_fs"),
            st("c_kp"), st("c_vp"), st("c_ks"), st("c_vs"), st("m_kp"), st("m_vp"))
```

```python
import functools

import numpy as np
import jax
import jax.numpy as jnp
from jax import lax
from jax.experimental import pallas as pl
from jax.experimental.pallas import tpu as pltpu

F32 = jnp.float32
BF16 = jnp.bfloat16

D_MODEL = 1024
HEAD_DIM = 64
LANES = 128
CHUNK = 64
A_LEFT_CHUNKS = 8
A_PAST = A_LEFT_CHUNKS * CHUNK
REL_CLIP = 128
N_HEADS_X = 4
HEAD_DIM_X = D_MODEL // N_HEADS_X
D_FF = 4 * D_MODEL
EPS = 1e-6
NEG = -1e30
QK_SCALE = HEAD_DIM ** -0.5
X_SCALE = HEAD_DIM_X ** -0.5

A_TQ = 256
A_WIN = A_PAST + A_TQ
ATT_T = 256
FF_CHUNK = 1024
VMEM_LIMIT = 56 * 1024 * 1024


def _row_tile(n, cap=512):
    t = cap
    while n % t:
        t //= 2
    return t


def _params(n_axes, vmem=None):
    return pltpu.CompilerParams(dimension_semantics=("arbitrary",) * n_axes,
                                vmem_limit_bytes=vmem)


def _const_spec(shape):
    nd = len(shape)
    return pl.BlockSpec(shape, lambda *_: (0,) * nd, pipeline_mode=pl.Buffered(1))


def _rms_rows(x, g):
    ms = jnp.mean(x * x, axis=-1, keepdims=True)
    return x * lax.rsqrt(ms + EPS) * g


def _log_sigmoid(x):
    return jnp.minimum(x, 0.0) - jnp.log1p(jnp.exp(-jnp.abs(x)))


def _lo_mask():
    return lax.broadcasted_iota(jnp.int32, (1, LANES), 1) < HEAD_DIM


def _pair_headnorm(blk, gain, lo):
    sq = blk * blk
    s_lo = jnp.sum(jnp.where(lo, sq, 0.0), axis=-1, keepdims=True)
    s_hi = jnp.sum(jnp.where(lo, 0.0, sq), axis=-1, keepdims=True)
    inv = jnp.where(lo, lax.rsqrt(s_lo * (1.0 / HEAD_DIM) + EPS),
                    lax.rsqrt(s_hi * (1.0 / HEAD_DIM) + EPS))
    return blk * inv * gain


def _dot_nt(a, b):
    return lax.dot_general(a, b, (((1,), (1,)), ((), ())), preferred_element_type=F32)


def _store_rows(ref, cols, blk, time_minor):
    if time_minor:
        ref[0, cols, :] = jnp.transpose(blk)
    else:
        ref[:, cols] = blk.astype(ref.dtype)


def _proj_ab_kernel(h_ref, g_ref, w_ref, wf_ref, bf_ref, gain_ref,
                    qa_ref, ka_ref, va_ref, qb_ref, kb_ref, vb_ref, lf_ref, *, time_minor):
    h = h_ref[0] if time_minor else h_ref[...]
    xn = _rms_rows(h, g_ref[...]).astype(BF16)
    lo = _lo_mask()
    seg_w = w_ref.shape[1] // 6
    outs = ((qa_ref, 0, True), (ka_ref, 1, False), (va_ref, None, False),
            (qb_ref, 2, True), (kb_ref, 3, False), (vb_ref, None, False))
    for s, (ref, gi, is_q) in enumerate(outs):
        y = jnp.dot(xn, w_ref[:, s * seg_w:(s + 1) * seg_w], preferred_element_type=F32)
        for p in range(seg_w // LANES):
            cols = slice(p * LANES, (p + 1) * LANES)
            blk = y[:, cols]
            if gi is not None:
                blk = _pair_headnorm(blk, gain_ref[gi:gi + 1, :], lo)
            if is_q:
                q_dst = ref.at[0] if time_minor else ref
                q_dst[:, cols] = (blk * QK_SCALE).astype(ref.dtype)
            else:
                _store_rows(ref, cols, blk, time_minor)
    lf = _log_sigmoid(jnp.dot(xn, wf_ref[...], preferred_element_type=F32) + bf_ref[...])
    if time_minor:
        lf_ref[0] = jnp.transpose(lf)[:lf_ref.shape[1], :]
    else:
        lf_ref[...] = lf


def _proj_ab(h, g, w_main, wf, bf, gains, n_forget):
    time_minor = h.ndim == 3
    dab = w_main.shape[1] // 6
    consts = [_const_spec(g.shape), _const_spec(w_main.shape), _const_spec(wf.shape),
              _const_spec(bf.shape), _const_spec(gains.shape)]
    if time_minor:
        b, s, d = h.shape
        ts = _row_tile(s)
        grid = (b, s // ts)
        nat = lambda w: pl.BlockSpec((1, ts, w), lambda bi, i: (bi, i, 0))
        tm = lambda w: pl.BlockSpec((1, w, ts), lambda bi, i: (bi, 0, i))
        in_specs = [nat(d)] + consts
        out_specs = [nat(dab), tm(dab), tm(dab), nat(dab), tm(dab), tm(dab), tm(n_forget)]
        sds = lambda shape, dt: jax.ShapeDtypeStruct(shape, dt)
        out_shape = (sds((b, s, dab), BF16), sds((b, dab, s), F32), sds((b, dab, s), F32),
                     sds((b, s, dab), BF16), sds((b, dab, s), F32), sds((b, dab, s), F32),
                     sds((b, n_forget, s), F32))
    else:
        n = h.shape[0]
        ts = _row_tile(n)
        grid = (n // ts,)
        row = lambda w: pl.BlockSpec((ts, w), lambda i: (i, 0))
        in_specs = [row(D_MODEL)] + consts
        out_specs = [row(dab)] * 6 + [row(LANES)]
        out_shape = tuple(jax.ShapeDtypeStruct((n, dab), dt) for dt in (BF16, F32, F32, BF16, F32, F32)) + (
            jax.ShapeDtypeStruct((n, LANES), F32),)
    return pl.pallas_call(
        functools.partial(_proj_ab_kernel, time_minor=time_minor),
        grid=grid,
        in_specs=in_specs,
        out_specs=out_specs,
        out_shape=out_shape,
        compiler_params=_params(len(grid), VMEM_LIMIT),
        name="proj_ab",
    )(h, g, w_main, wf, bf, gains)


def _proj_sb_kernel(h_ref, g_ref, w_ref, q_ref, k_ref, v_ref, *, time_minor):
    h = h_ref[0] if time_minor else h_ref[...]
    xn = _rms_rows(h, g_ref[...]).astype(BF16)
    dc = w_ref.shape[1] // 3
    q = jnp.dot(xn, w_ref[:, :dc], preferred_element_type=F32)
    q_dst = q_ref.at[0] if time_minor else q_ref
    q_dst[...] = (q * QK_SCALE).astype(BF16)
    for ref, c0 in ((k_ref, dc), (v_ref, 2 * dc)):
        y = jnp.dot(xn, w_ref[:, c0:c0 + dc], preferred_element_type=F32)
        for p in range(dc // LANES):
            cols = slice(p * LANES, (p + 1) * LANES)
            _store_rows(ref, cols, y[:, cols], time_minor)


def _proj_sb(h, g, w):
    time_minor = h.ndim == 3
    dc = w.shape[1] // 3
    consts = [_const_spec(g.shape), _const_spec(w.shape)]
    if time_minor:
        b, s, d = h.shape
        ts = _row_tile(s)
        grid = (b, s // ts)
        nat = lambda width: pl.BlockSpec((1, ts, width), lambda bi, i: (bi, i, 0))
        tm = pl.BlockSpec((1, dc, ts), lambda bi, i: (bi, 0, i))
        in_specs = [nat(d)] + consts
        out_specs = [nat(dc), tm, tm]
        out_shape = (jax.ShapeDtypeStruct((b, s, dc), BF16), jax.ShapeDtypeStruct((b, dc, s), F32),
                     jax.ShapeDtypeStruct((b, dc, s), F32))
    else:
        n = h.shape[0]
        ts = _row_tile(n)
        grid = (n // ts,)
        row = lambda width: pl.BlockSpec((ts, width), lambda i: (i, 0))
        in_specs = [row(D_MODEL)] + consts
        out_specs = [row(dc)] * 3
        out_shape = (jax.ShapeDtypeStruct((n, dc), BF16), jax.ShapeDtypeStruct((n, dc), F32),
                     jax.ShapeDtypeStruct((n, dc), F32))
    return pl.pallas_call(
        functools.partial(_proj_sb_kernel, time_minor=time_minor),
        grid=grid,
        in_specs=in_specs,
        out_specs=out_specs,
        out_shape=out_shape,
        compiler_params=_params(len(grid), VMEM_LIMIT),
        name="proj_sb",
    )(h, g, w)


def _split3(x):
    hi = x.astype(BF16)
    r = x - hi.astype(F32)
    mid = r.astype(BF16)
    lo = (r - mid.astype(F32)).astype(BF16)
    return hi, mid, lo


def _cumsum_kernel(x_ref, o_ref):
    rows, length = x_ref.shape
    r_i = lax.broadcasted_iota(jnp.int32, (LANES, LANES), 0)
    c_i = lax.broadcasted_iota(jnp.int32, (LANES, LANES), 1)
    tri = jnp.where(r_i <= c_i, 1.0, 0.0).astype(BF16)
    carry = jnp.zeros((rows, 1), F32)
    for b in range(length // LANES):
        hi, mid, lo = _split3(x_ref[:, b * LANES:(b + 1) * LANES])
        c = (jnp.dot(hi, tri, preferred_element_type=F32)
             + jnp.dot(mid, tri, preferred_element_type=F32)
             + jnp.dot(lo, tri, preferred_element_type=F32)) + carry
        o_ref[:, b * LANES:(b + 1) * LANES] = c
        carry = c[:, LANES - 1:LANES]


def _cumsum_rows(x):
    return pl.pallas_call(
        _cumsum_kernel,
        out_shape=jax.ShapeDtypeStruct(x.shape, F32),
        name="cumsum_logf",
    )(x)


def _prep_kv(k_ref, v_ref, kt, vb, time_minor):
    if time_minor:
        kt[...] = k_ref[0].astype(BF16)
        vb[...] = jnp.transpose(v_ref[0]).astype(BF16)
    else:
        kt[...] = jnp.transpose(k_ref[0]).astype(BF16)
        vb[...] = v_ref[0].astype(BF16)


def _stack_heads(q, lo):
    qf = q.astype(F32)
    return jnp.concatenate([jnp.where(lo, qf, 0.0), jnp.where(lo, 0.0, qf)], axis=0).astype(BF16)


def _unstack_heads(o2, lo):
    t = o2.shape[0] // 2
    return jnp.where(lo, o2[:t], o2[t:])


def _softmax_pv(pieces):
    m = None
    for x, _ in pieces:
        mx = jnp.max(x, axis=-1, keepdims=True)
        m = mx if m is None else jnp.maximum(m, mx)
    l, o = None, None
    for x, vv in pieces:
        p = jnp.exp(x - m)
        ls = jnp.sum(p, axis=-1, keepdims=True)
        os_ = jnp.dot(p.astype(BF16), vv, preferred_element_type=F32)
        l = ls if l is None else l + ls
        o = os_ if o is None else o + os_
    return o / l


def _stacked_causal(t):
    r = lax.broadcasted_iota(jnp.int32, (2 * t, t), 0)
    r = jnp.where(r >= t, r - t, r)
    c = lax.broadcasted_iota(jnp.int32, (2 * t, t), 1)
    return c <= r, c < r


def _attn_a_prompt_kernel(q_ref, k_ref, v_ref, bias_ref, o_ref, kt, vb, *, nq, win, ncase):
    lo = _lo_mask()
    _prep_kv(k_ref, v_ref, kt, vb, True)
    for i in range(nq):
        q0 = i * A_TQ
        case = min(i, ncase - 1)
        w0 = max(q0 + A_TQ - win, 0)
        wlen = min(win, q0 + A_TQ)
        q2 = _stack_heads(q_ref[0, q0:q0 + A_TQ, :], lo)
        x = jnp.dot(q2, kt[:, w0:w0 + wlen], preferred_element_type=F32) + bias_ref[0, case, :, :wlen]
        o2 = _softmax_pv([(x, vb[w0:w0 + wlen, :])])
        o_ref[0, q0:q0 + A_TQ, :] = _unstack_heads(o2, lo).astype(o_ref.dtype)


def _toeplitz_bias(rel_bias, rows, cols, shift):
    p = rows + cols - 1
    idx = np.clip(np.arange(p) - (rows - 1) + shift, -REL_CLIP, REL_CLIP) + REL_CLIP
    e = jnp.transpose(rel_bias[idx]).astype(F32)
    h = e.shape[0]
    skew = jnp.tile(e, (1, rows + 1))[:, :rows * (p + 1)].reshape(h, rows, p + 1)
    return skew[:, ::-1, :cols]


def _band_bias_prompt(rel_bias, s_len):
    win = min(A_WIN, s_len)
    nq = s_len // A_TQ
    ncase = min(nq, A_PAST // A_TQ + 1)
    r = np.arange(A_TQ)[:, None]
    j = np.arange(win)[None, :]
    tabs = []
    for c in range(ncase):
        q_off = c * A_TQ if (c + 1) * A_TQ <= win else win - A_TQ
        qc, kc = (q_off + r) // CHUNK, j // CHUNK
        ok = (kc <= qc) & (kc >= qc - A_LEFT_CHUNKS)
        tabs.append(jnp.where(ok[None], _toeplitz_bias(rel_bias, A_TQ, win, -q_off), NEG))
    tab = jnp.stack(tabs, axis=1)
    h = tab.shape[0]
    tab = jnp.transpose(tab.reshape(h // 2, 2, ncase, A_TQ, win), (0, 2, 1, 3, 4))
    return tab.reshape(h // 2, ncase, 2 * A_TQ, win), win, ncase


def _attn_a_prompt(q, k_t, v_t, rel_bias):
    b, s, da = q.shape
    pairs = da // LANES
    bias, win, ncase = _band_bias_prompt(rel_bias, s)
    nq = s // A_TQ
    blk = lambda: pl.BlockSpec((1, s, LANES), lambda bi, p: (bi, 0, p))
    blk_t = lambda: pl.BlockSpec((1, LANES, s), lambda bi, p: (bi, p, 0))
    return pl.pallas_call(
        functools.partial(_attn_a_prompt_kernel, nq=nq, win=win, ncase=ncase),
        grid=(b, pairs),
        in_specs=[blk(), blk_t(), blk_t(),
                  pl.BlockSpec((1, ncase, 2 * A_TQ, win), lambda bi, p: (p, 0, 0, 0))],
        out_specs=blk(),
        out_shape=jax.ShapeDtypeStruct((b, s, da), BF16),
        scratch_shapes=[pltpu.VMEM((LANES, s), BF16), pltpu.VMEM((s, LANES), BF16)],
        compiler_params=_params(2, VMEM_LIMIT),
        name="attn_a_prompt",
    )(q, k_t, v_t, bias)


def _attn_a_sample_kernel(q_ref, kc_ref, vc_ref, kn_ref, vn_ref, bc_ref, bn_ref, o_ref):
    lo = _lo_mask()
    q2 = _stack_heads(q_ref[0], lo)
    xc = jnp.dot(q2, kc_ref[0].astype(BF16), preferred_element_type=F32) + bc_ref[0]
    xn = _dot_nt(q2, kn_ref[0].astype(BF16)) + bn_ref[0]
    o2 = _softmax_pv([(xc, jnp.transpose(vc_ref[0]).astype(BF16)), (xn, vn_ref[0].astype(BF16))])
    o_ref[0] = _unstack_heads(o2, lo).astype(o_ref.dtype)


def _attn_a_sample(q, k_new, v_new, k_cache, v_cache, rel_bias, past):
    b, t, da = q.shape
    pairs = da // LANES
    n_band = k_cache.shape[2]
    q_pos = past + np.arange(t)[:, None]
    k_pos = past - n_band + np.arange(n_band + t)[None, :]
    qc, kc = q_pos // CHUNK, k_pos // CHUNK
    ok = (k_pos >= 0) & (kc <= qc) & (kc >= qc - A_LEFT_CHUNKS)
    tab = jnp.where(ok[None], _toeplitz_bias(rel_bias, t, n_band + t, -n_band), NEG)
    tab = tab.reshape(pairs, 2 * t, n_band + t)
    bias_c, bias_n = tab[..., :n_band], tab[..., n_band:]
    new = lambda: pl.BlockSpec((1, t, LANES), lambda bi, p: (bi, 0, p))
    old = lambda: pl.BlockSpec((1, LANES, n_band), lambda bi, p: (bi, p, 0))
    return pl.pallas_call(
        _attn_a_sample_kernel,
        grid=(b, pairs),
        in_specs=[new(), old(), old(), new(), new(),
                  pl.BlockSpec((1, 2 * t, n_band), lambda bi, p: (p, 0, 0)),
                  pl.BlockSpec((1, 2 * t, t), lambda bi, p: (p, 0, 0))],
        out_specs=new(),
        out_shape=jax.ShapeDtypeStruct((b, t, da), BF16),
        compiler_params=_params(2, VMEM_LIMIT),
        name="attn_a_sample",
    )(q, k_cache, v_cache, k_new, v_new, bias_c, bias_n)


def _attn_b_kernel(*refs, tq, nq, plen, time_minor):
    if plen:
        (q_ref, k_ref, v_ref, crow_ref, ccol_ref, kp_ref, vp_ref, cpre_ref, o_ref, kt, vb, kpt, vpb) = refs
    else:
        q_ref, k_ref, v_ref, crow_ref, ccol_ref, o_ref, kt, vb = refs
    pair = pl.program_id(1)
    lo = _lo_mask()
    _prep_kv(k_ref, v_ref, kt, vb, time_minor)
    if plen:
        _prep_kv(kp_ref, vp_ref, kpt, vpb, True)
    keep, _ = _stacked_causal(tq)
    head_lane = lax.broadcasted_iota(jnp.int32, (1, ccol_ref.shape[2]), 1)

    def logits(q2, kt_cols, cq, ck_ref, c0, c1):
        z = jnp.dot(q2, kt_cols, preferred_element_type=F32)
        return jnp.concatenate([z[:tq] + (cq[0] - ck_ref[0, 0, 0:1, c0:c1]),
                                z[tq:] + (cq[1] - ck_ref[0, 0, 1:2, c0:c1])], axis=0)

    for i in range(nq):
        d0, d1 = i * tq, (i + 1) * tq
        q2 = _stack_heads(q_ref[0, d0:d1, :], lo)
        cq_all = ccol_ref[0, d0:d1, :]
        cq = [jnp.sum(jnp.where(head_lane == 2 * pair + e, cq_all, 0.0), axis=-1, keepdims=True)
              for e in range(2)]
        pieces = []
        if plen:
            pieces.append((logits(q2, kpt[...], cq, cpre_ref, 0, plen), vpb[...]))
        if i:
            pieces.append((logits(q2, kt[:, 0:d0], cq, crow_ref, 0, d0), vb[0:d0, :]))
        xd = logits(q2, kt[:, d0:d1], cq, crow_ref, d0, d1)
        pieces.append((jnp.where(keep, xd, NEG), vb[d0:d1, :]))
        o_ref[0, d0:d1, :] = _unstack_heads(_softmax_pv(pieces), lo).astype(o_ref.dtype)


def _attn_b(q, k, v, c_self, k_pre=None, v_pre=None, c_pre=None, *, time_minor):
    b, t, db = q.shape
    pairs = db // LANES
    n_heads = c_self.shape[1]
    tq = min(ATT_T, t)
    c_col = jnp.transpose(c_self, (0, 2, 1))
    new = lambda: pl.BlockSpec((1, t, LANES), lambda bi, p: (bi, 0, p))
    tm = lambda n: pl.BlockSpec((1, LANES, n), lambda bi, p: (bi, p, 0))
    kv = (lambda: tm(t)) if time_minor else new
    in_specs = [new(), kv(), kv(),
                pl.BlockSpec((1, 1, 2, t), lambda bi, p: (bi, p, 0, 0)),
                pl.BlockSpec((1, t, n_heads), lambda bi, p: (bi, 0, 0))]
    args = [q, k, v, c_self.reshape(b, pairs, 2, t), c_col]
    scratch = [pltpu.VMEM((LANES, t), BF16), pltpu.VMEM((t, LANES), BF16)]
    plen = 0
    if k_pre is not None:
        plen = k_pre.shape[2]
        in_specs += [tm(plen), tm(plen), pl.BlockSpec((1, 1, 2, plen), lambda bi, p: (bi, p, 0, 0))]
        args += [k_pre, v_pre, c_pre.reshape(b, pairs, 2, plen)]
        scratch += [pltpu.VMEM((LANES, plen), BF16), pltpu.VMEM((plen, LANES), BF16)]
    return pl.pallas_call(
        functools.partial(_attn_b_kernel, tq=tq, nq=t // tq, plen=plen, time_minor=time_minor),
        grid=(b, pairs),
        in_specs=in_specs,
        out_specs=new(),
        out_shape=jax.ShapeDtypeStruct((b, t, db), BF16),
        scratch_shapes=scratch,
        compiler_params=_params(2, VMEM_LIMIT),
        name="attn_b",
    )(*args)


def _later_ones(t):
    r = lax.broadcasted_iota(jnp.int32, (t, t), 0)
    c = lax.broadcasted_iota(jnp.int32, (t, t), 1)
    return jnp.where(r > c, 1.0, 0.0).astype(BF16)


def _stick_weights(z, blk, ones_ref, carry, before):
    neg_abs = pltpu.bitcast(pltpu.bitcast(z, jnp.uint32) | jnp.uint32(0x80000000), F32)
    soft = jnp.maximum(z, 0.0) + jnp.log(1.0 + jnp.exp(neg_abs))
    nblk = z.shape[1] // blk
    ws = [None] * nblk
    for j in reversed(range(nblk)):
        cols = slice(j * blk, (j + 1) * blk)
        masked = before is not None and j == nblk - 1
        sp = soft[:, cols]
        keep = jnp.where(before, sp, 0.0) if masked else sp
        later = jnp.dot(keep.astype(BF16), ones_ref[...], preferred_element_type=F32)
        w = jnp.exp((z[:, cols] - sp) - (later + carry))
        if masked:
            w = jnp.where(before, w, 0.0)
        ws[j] = w.astype(BF16)
        carry = carry + (later[:, 0:1] + keep[:, 0:1])
    return (ws[0] if nblk == 1 else jnp.concatenate(ws, axis=-1)), carry


def _attn_c_kernel(*refs, tq, nq, plen, tpre, time_minor):
    if plen:
        q_ref, k_ref, v_ref, kp_ref, vp_ref, o_ref, kt, vb, u_self, kpt, vpb, u_pre = refs
    else:
        q_ref, k_ref, v_ref, o_ref, kt, vb, u_self = refs
    lo = _lo_mask()
    _prep_kv(k_ref, v_ref, kt, vb, time_minor)
    u_self[...] = _later_ones(tq)
    if plen:
        _prep_kv(kp_ref, vp_ref, kpt, vpb, True)
        u_pre[...] = _later_ones(tpre)
    _, before = _stacked_causal(tq)
    for i in range(nq):
        d0, d1 = i * tq, (i + 1) * tq
        q2 = _stack_heads(q_ref[0, d0:d1, :], lo)
        z = jnp.dot(q2, kt[:, 0:d1], preferred_element_type=F32)
        w, carry = _stick_weights(z, tq, u_self, jnp.zeros((2 * tq, 1), F32), before)
        o2 = jnp.dot(w, vb[0:d1, :], preferred_element_type=F32)
        if plen:
            wp, _ = _stick_weights(jnp.dot(q2, kpt[...], preferred_element_type=F32), tpre, u_pre, carry, None)
            o2 = o2 + jnp.dot(wp, vpb[...], preferred_element_type=F32)
        o_ref[0, d0:d1, :] = _unstack_heads(o2, lo).astype(o_ref.dtype)


def _attn_c(q, k, v, k_pre=None, v_pre=None, *, time_minor):
    b, t, dc = q.shape
    pairs = dc // LANES
    tq = min(ATT_T, t)
    new = lambda: pl.BlockSpec((1, t, LANES), lambda bi, p: (bi, 0, p))
    tm = lambda n: pl.BlockSpec((1, LANES, n), lambda bi, p: (bi, p, 0))
    kv = (lambda: tm(t)) if time_minor else new
    in_specs = [new(), kv(), kv()]
    args = [q, k, v]
    scratch = [pltpu.VMEM((LANES, t), BF16), pltpu.VMEM((t, LANES), BF16), pltpu.VMEM((tq, tq), BF16)]
    plen = tpre = 0
    if k_pre is not None:
        plen = k_pre.shape[2]
        tpre = min(ATT_T, plen)
        in_specs += [tm(plen), tm(plen)]
        args += [k_pre, v_pre]
        scratch += [pltpu.VMEM((LANES, plen), BF16), pltpu.VMEM((plen, LANES), BF16),
                    pltpu.VMEM((tpre, tpre), BF16)]
    return pl.pallas_call(
        functools.partial(_attn_c_kernel, tq=tq, nq=t // tq, plen=plen, tpre=tpre, time_minor=time_minor),
        grid=(b, pairs),
        in_specs=in_specs,
        out_specs=new(),
        out_shape=jax.ShapeDtypeStruct((b, t, dc), BF16),
        scratch_shapes=scratch,
        compiler_params=_params(2, VMEM_LIMIT),
        name="attn_c",
    )(*args)


def _softmax_rows(x):
    m = jnp.max(x, axis=-1, keepdims=True)
    p = jnp.exp(x - m)
    return p, jnp.sum(p, axis=-1, keepdims=True)


def _head_norm_x(y, gain):
    parts = []
    for h in range(N_HEADS_X):
        blk = y[:, h * HEAD_DIM_X:(h + 1) * HEAD_DIM_X]
        ms = jnp.mean(blk * blk, axis=-1, keepdims=True)
        parts.append(blk * lax.rsqrt(ms + EPS) * gain)
    return parts


def _mem_kv_kernel(m_ref, g_ref, w_ref, gain_ref, k_ref, v_ref, kb_ref, vb_ref):
    xn = _rms_rows(m_ref[...], g_ref[...]).astype(BF16)
    dx = w_ref.shape[1] // 2
    k = jnp.dot(xn, w_ref[:, :dx], preferred_element_type=F32)
    for h, blk in enumerate(_head_norm_x(k, gain_ref[...])):
        k_ref[:, h * HEAD_DIM_X:(h + 1) * HEAD_DIM_X] = blk
        kb_ref[:, h * HEAD_DIM_X:(h + 1) * HEAD_DIM_X] = blk.astype(BF16)
    v = jnp.dot(xn, w_ref[:, dx:], preferred_element_type=F32)
    v_ref[...] = v
    vb_ref[...] = v.astype(BF16)


def _mem_kv(mem2d, g, w_kv, k_gain):
    n = mem2d.shape[0]
    ts = _row_tile(n)
    dx = w_kv.shape[1] // 2
    row = lambda width: pl.BlockSpec((ts, width), lambda i: (i, 0))
    return pl.pallas_call(
        _mem_kv_kernel,
        grid=(n // ts,),
        in_specs=[row(D_MODEL), _const_spec(g.shape), _const_spec(w_kv.shape), _const_spec(k_gain.shape)],
        out_specs=[row(dx)] * 4,
        out_shape=(jax.ShapeDtypeStruct((n, dx), F32), jax.ShapeDtypeStruct((n, dx), F32),
                   jax.ShapeDtypeStruct((n, dx), BF16), jax.ShapeDtypeStruct((n, dx), BF16)),
        compiler_params=_params(1, VMEM_LIMIT),
        name="mem_kv",
    )(mem2d, g, w_kv, k_gain)


def _post_kernel(h_ref, o1_ref, o2_ref, wout_ref, g_ref, wq_ref, qgain_ref, mk_ref, mv_ref, wo_ref, y_ref):
    half = o1_ref.shape[2]
    h1 = (h_ref[0]
          + jnp.dot(o1_ref[0], wout_ref[:half, :], preferred_element_type=F32)
          + jnp.dot(o2_ref[0], wout_ref[half:, :], preferred_element_type=F32))
    hn = _rms_rows(h1, g_ref[...]).astype(BF16)
    qx = jnp.dot(hn, wq_ref[...], preferred_element_type=F32)
    heads = []
    for h, qh in enumerate(_head_norm_x(qx, qgain_ref[...])):
        cols = slice(h * HEAD_DIM_X, (h + 1) * HEAD_DIM_X)
        s = _dot_nt((qh * X_SCALE).astype(BF16), mk_ref[0, :, cols])
        p, l = _softmax_rows(s)
        oh = jnp.dot(p.astype(BF16), mv_ref[0, :, cols], preferred_element_type=F32) / l
        heads.append(oh.astype(BF16))
    y_ref[0] = h1 + jnp.dot(jnp.concatenate(heads, axis=-1), wo_ref[...], preferred_element_type=F32)


def _post(h, o1, o2, w_out, g_cross, w_q, q_gain, mk, mv, w_o):
    b, s, d = h.shape
    ts = _row_tile(s)
    half = w_out.shape[0] // 2
    n_mem = mk.shape[1]
    tile = lambda width: pl.BlockSpec((1, ts, width), lambda bi, i: (bi, i, 0))
    o1_spec = pl.BlockSpec((1, ts, half), lambda bi, i: (bi, i, 0))
    if o2 is None:
        o2 = o1
        o2_spec = pl.BlockSpec((1, ts, half), lambda bi, i: (bi, i, 1))
    else:
        o2_spec = o1_spec
    mem = lambda: pl.BlockSpec((1, n_mem, d), lambda bi, i: (bi, 0, 0))
    return pl.pallas_call(
        _post_kernel,
        grid=(b, s // ts),
        in_specs=[tile(d), o1_spec, o2_spec, _const_spec(w_out.shape), _const_spec(g_cross.shape),
                  _const_spec(w_q.shape), _const_spec(q_gain.shape), mem(), mem(), _const_spec(w_o.shape)],
        out_specs=tile(d),
        out_shape=jax.ShapeDtypeStruct((b, s, d), F32),
        compiler_params=_params(2, VMEM_LIMIT),
        name="post_mixer_cross",
    )(h, o1, o2, w_out, g_cross, w_q, q_gain, mk, mv, w_o)


def _mlp_kernel(h_ref, g_ref, wup_ref, wdn_ref, y_ref):
    h = h_ref[...]
    hn = _rms_rows(h, g_ref[...]).astype(BF16)
    acc = h
    for c in range(wup_ref.shape[1] // FF_CHUNK):
        cols = slice(c * FF_CHUNK, (c + 1) * FF_CHUNK)
        u = jnp.maximum(jnp.dot(hn, wup_ref[:, cols], preferred_element_type=F32), 0.0)
        acc = acc + jnp.dot((u * u).astype(BF16), wdn_ref[cols, :], preferred_element_type=F32)
    y_ref[...] = acc


def _mlp(h2d, g, w_up, w_down):
    n = h2d.shape[0]
    ts = _row_tile(n)
    row = pl.BlockSpec((ts, D_MODEL), lambda i: (i, 0))
    return pl.pallas_call(
        _mlp_kernel,
        grid=(n // ts,),
        in_specs=[row, _const_spec(g.shape), _const_spec(w_up.shape), _const_spec(w_down.shape)],
        out_specs=row,
        out_shape=jax.ShapeDtypeStruct((n, D_MODEL), F32),
        compiler_params=_params(1, VMEM_LIMIT),
        name="mlp",
    )(h2d, g, w_up, w_down)


def _row_vec(x):
    return x.reshape(1, -1).astype(F32)


def _ab_weights(w_in, b_f, qk_gain):
    dab3 = w_in.shape[1] - b_f.shape[0]
    n_f = b_f.shape[0]
    w_main = w_in[:, :dab3].astype(BF16)
    wf = jnp.zeros((w_in.shape[0], LANES), F32).at[:, :n_f].set(w_in[:, dab3:]).astype(BF16)
    bf = jnp.zeros((1, LANES), F32).at[0, :n_f].set(b_f)
    gains = jnp.concatenate([qk_gain, qk_gain], axis=-1).astype(F32)
    return w_main, wf, bf, gains


def _time_minor(x):
    b, t = x.shape[:2]
    return jnp.transpose(x.reshape(b, t, -1), (0, 2, 1))


def _heads_out(x_t, n_heads):
    b, _, t = x_t.shape
    return jnp.transpose(x_t, (0, 2, 1)).reshape(b, t, n_heads, HEAD_DIM)


def kernel(x_prompt, x_sample, mem_prompt, cache_a_k, cache_a_v, cache_b_k, cache_b_v, cache_b_logf,
           cache_c_k, cache_c_v, cache_mem_k, cache_mem_v, norm_mix, norm_cross, norm_mlp, norm_mem,
           ab_w_in, ab_forget_bias, ab_qk_gain, ab_rel_bias, ab_w_out, sb_w_in, sb_w_out,
           x_w_q, x_w_kv, x_qk_gain, x_w_o, mlp_w_up, mlp_w_down):
    bp, sp, d = x_prompt.shape
    bs, ss, _ = x_sample.shape
    depth = norm_mix.shape[0]
    n_mem = mem_prompt.shape[1]
    hb = ab_forget_bias.shape[1]
    hp, hs = x_prompt, x_sample
    outs = {name: [] for name in ("a_kp", "a_vp", "a_ks", "a_vs", "b_kp", "b_vp", "b_fp", "b_ks", "b_vs",
                                  "b_fs", "c_kp", "c_vp", "c_ks", "c_vs", "m_kp", "m_vp")}
    mem2d = mem_prompt.reshape(bp * n_mem, d)

    for layer in range(depth):
        i = layer // 2
        g_mix = _row_vec(norm_mix[layer])
        if layer % 2 == 0:
            w_main, wf, bf, gains = _ab_weights(ab_w_in[i], ab_forget_bias[i], ab_qk_gain[i])
            da = w_main.shape[1] // 6
            ha = da // HEAD_DIM
            qa, ka, va, qb, kb, vb, lf = _proj_ab(hp, g_mix, w_main, wf, bf, gains, hb)
            c_p = _cumsum_rows(lf.reshape(bp * hb, sp)).reshape(bp, hb, sp)
            oa = _attn_a_prompt(qa, ka, va, ab_rel_bias[i])
            ob = _attn_b(qb, kb, vb, c_p, time_minor=True)
            keep = min(A_PAST, sp)
            outs["a_kp"].append(_heads_out(ka[:, :, sp - keep:], ha))
            outs["a_vp"].append(_heads_out(va[:, :, sp - keep:], ha))
            outs["b_kp"].append(_heads_out(kb, hb))
            outs["b_vp"].append(_heads_out(vb, hb))
            outs["b_fp"].append(jnp.transpose(lf, (0, 2, 1)))
            mix_p = (oa, ob)
            qa, ka, va, qb, kb, vb, lf = _proj_ab(hs.reshape(bs * ss, d), g_mix, w_main, wf, bf, gains, hb)
            r3 = lambda x: x.reshape(bs, ss, x.shape[-1])
            qa, ka, va, qb, kb, vb = (r3(x) for x in (qa, ka, va, qb, kb, vb))
            logf = r3(lf)[..., :hb]
            past = cache_b_k.shape[2]
            heads_first = lambda x: jnp.transpose(x, (0, 2, 1))
            lf_all = jnp.concatenate([heads_first(cache_b_logf[i].astype(F32)), heads_first(logf)], axis=-1)
            total = past + ss
            padded = -(-total // LANES) * LANES
            lf_all = jnp.pad(lf_all, ((0, 0), (0, 0), (0, padded - total)))
            c_all = _cumsum_rows(lf_all.reshape(bs * hb, padded)).reshape(bs, hb, padded)
            oa = _attn_a_sample(qa, ka, va, _time_minor(cache_a_k[i]), _time_minor(cache_a_v[i]),
                                ab_rel_bias[i], past)
            ob = _attn_b(qb, kb, vb, c_all[..., past:total], _time_minor(cache_b_k[i]), _time_minor(cache_b_v[i]),
                         c_all[..., :past], time_minor=False)
            outs["a_ks"].append(ka.reshape(bs, ss, ha, HEAD_DIM))
            outs["a_vs"].append(va.reshape(bs, ss, ha, HEAD_DIM))
            outs["b_ks"].append(kb.reshape(bs, ss, hb, HEAD_DIM))
            outs["b_vs"].append(vb.reshape(bs, ss, hb, HEAD_DIM))
            outs["b_fs"].append(logf)
            mix_s = (oa, ob)
            w_out = ab_w_out[i].astype(BF16)
        else:
            w_sb = sb_w_in[i].astype(BF16)
            dc = w_sb.shape[1] // 3
            hc = dc // HEAD_DIM
            q, k, v = _proj_sb(hp, g_mix, w_sb)
            mix_p = (_attn_c(q, k, v, time_minor=True), None)
            outs["c_kp"].append(_heads_out(k, hc))
            outs["c_vp"].append(_heads_out(v, hc))
            q, k, v = _proj_sb(hs.reshape(bs * ss, d), g_mix, w_sb)
            q, k, v = (x.reshape(bs, ss, dc) for x in (q, k, v))
            mix_s = (_attn_c(q, k, v, _time_minor(cache_c_k[i]), _time_minor(cache_c_v[i]), time_minor=False),
                     None)
            outs["c_ks"].append(k.reshape(bs, ss, hc, HEAD_DIM))
            outs["c_vs"].append(v.reshape(bs, ss, hc, HEAD_DIM))
            w_out = sb_w_out[i].astype(BF16)

        mk, mv, mkb, mvb = _mem_kv(mem2d, _row_vec(norm_mem[layer]), x_w_kv[layer].astype(BF16),
                                   _row_vec(x_qk_gain[layer, 1]))
        outs["m_kp"].append(mk.reshape(bp, n_mem, N_HEADS_X, HEAD_DIM_X))
        outs["m_vp"].append(mv.reshape(bp, n_mem, N_HEADS_X, HEAD_DIM_X))
        g_cross = _row_vec(norm_cross[layer])
        w_q, w_o = x_w_q[layer].astype(BF16), x_w_o[layer].astype(BF16)
        q_gain = _row_vec(x_qk_gain[layer, 0])
        hp = _post(hp, mix_p[0], mix_p[1], w_out, g_cross, w_q, q_gain,
                   mkb.reshape(bp, n_mem, d), mvb.reshape(bp, n_mem, d), w_o)
        hs = _post(hs, mix_s[0], mix_s[1], w_out, g_cross, w_q, q_gain,
                   cache_mem_k[layer].reshape(bs, n_mem, d).astype(BF16),
                   cache_mem_v[layer].reshape(bs, n_mem, d).astype(BF16), w_o)
        g_mlp = _row_vec(norm_mlp[layer])
        w_up, w_dn = mlp_w_up[layer].astype(BF16), mlp_w_down[layer].astype(BF16)
        hp = _mlp(hp.reshape(bp * sp, d), g_mlp, w_up, w_dn).reshape(bp, sp, d)
        hs = _mlp(hs.reshape(bs * ss, d), g_mlp, w_up, w_dn).reshape(bs, ss, d)

    st = lambda name: jnp.stack(outs[name])
    return (hp, hs, st("a_kp"), st("a_vp"), st("a_ks"), st("a_vs"),
            st("b_kp"), st("b_vp"), st("b_fp"), st("b_ks"), st("b_vs"), st("b_fs"),
            st("c_kp"), st("c_vp"), st("c_ks"), st("c_vs"), st("m_kp"), st("m_vp"))
```

```python
import functools

import numpy as np
import jax
import jax.numpy as jnp
from jax import lax
from jax.experimental import pallas as pl
from jax.experimental.pallas import tpu as pltpu

F32 = jnp.float32
BF16 = jnp.bfloat16

D_MODEL = 1024
HEAD_DIM = 64
LANES = 128
CHUNK = 64
A_LEFT_CHUNKS = 8
A_PAST = A_LEFT_CHUNKS * CHUNK
REL_CLIP = 128
N_HEADS_X = 4
HEAD_DIM_X = D_MODEL // N_HEADS_X
D_FF = 4 * D_MODEL
EPS = 1e-6
NEG = -1e30
QK_SCALE = HEAD_DIM ** -0.5
X_SCALE = HEAD_DIM_X ** -0.5

A_TQ = 256
A_WIN = A_PAST + A_TQ
ATT_T = 256
FF_CHUNK = 1024
VMEM_LIMIT = 56 * 1024 * 1024


def _row_tile(n, cap=512):
    t = cap
    while n % t:
        t //= 2
    return t


def _params(n_axes, vmem=None):
    return pltpu.CompilerParams(dimension_semantics=("arbitrary",) * n_axes,
                                vmem_limit_bytes=vmem)


def _const_spec(shape):
    nd = len(shape)
    return pl.BlockSpec(shape, lambda *_: (0,) * nd, pipeline_mode=pl.Buffered(1))


def _rms_rows(x, g):
    ms = jnp.mean(x * x, axis=-1, keepdims=True)
    return x * lax.rsqrt(ms + EPS) * g


def _log_sigmoid(x):
    return jnp.minimum(x, 0.0) - jnp.log1p(jnp.exp(-jnp.abs(x)))


def _lo_mask():
    return lax.broadcasted_iota(jnp.int32, (1, LANES), 1) < HEAD_DIM


def _pair_headnorm(blk, gain, lo):
    sq = blk * blk
    s_lo = jnp.sum(jnp.where(lo, sq, 0.0), axis=-1, keepdims=True)
    s_hi = jnp.sum(jnp.where(lo, 0.0, sq), axis=-1, keepdims=True)
    inv = jnp.where(lo, lax.rsqrt(s_lo * (1.0 / HEAD_DIM) + EPS),
                    lax.rsqrt(s_hi * (1.0 / HEAD_DIM) + EPS))
    return blk * inv * gain


def _dot_nt(a, b):
    return lax.dot_general(a, b, (((1,), (1,)), ((), ())), preferred_element_type=F32)


def _store_rows(ref, cols, blk, time_minor):
    if time_minor:
        ref[0, cols, :] = jnp.transpose(blk)
    else:
        ref[:, cols] = blk.astype(ref.dtype)


def _proj_ab_kernel(h_ref, g_ref, w_ref, wf_ref, bf_ref, gain_ref,
                    qa_ref, ka_ref, va_ref, qb_ref, kb_ref, vb_ref, lf_ref, *, time_minor):
    h = h_ref[0] if time_minor else h_ref[...]
    xn = _rms_rows(h, g_ref[...]).astype(BF16)
    lo = _lo_mask()
    seg_w = w_ref.shape[1] // 6
    outs = ((qa_ref, 0, True), (ka_ref, 1, False), (va_ref, None, False),
            (qb_ref, 2, True), (kb_ref, 3, False), (vb_ref, None, False))
    for s, (ref, gi, is_q) in enumerate(outs):
        y = jnp.dot(xn, w_ref[:, s * seg_w:(s + 1) * seg_w], preferred_element_type=F32)
        for p in range(seg_w // LANES):
            cols = slice(p * LANES, (p + 1) * LANES)
            blk = y[:, cols]
            if gi is not None:
                blk = _pair_headnorm(blk, gain_ref[gi:gi + 1, :], lo)
            if is_q:
                q_dst = ref.at[0] if time_minor else ref
                q_dst[:, cols] = (blk * QK_SCALE).astype(ref.dtype)
            else:
                _store_rows(ref, cols, blk, time_minor)
    lf = _log_sigmoid(jnp.dot(xn, wf_ref[...], preferred_element_type=F32) + bf_ref[...])
    if time_minor:
        lf_ref[0] = jnp.transpose(lf)[:lf_ref.shape[1], :]
    else:
        lf_ref[...] = lf


def _proj_ab(h, g, w_main, wf, bf, gains, n_forget):
    time_minor = h.ndim == 3
    dab = w_main.shape[1] // 6
    consts = [_const_spec(g.shape), _const_spec(w_main.shape), _const_spec(wf.shape),
              _const_spec(bf.shape), _const_spec(gains.shape)]
    if time_minor:
        b, s, d = h.shape
        ts = _row_tile(s)
        grid = (b, s // ts)
        nat = lambda w: pl.BlockSpec((1, ts, w), lambda bi, i: (bi, i, 0))
        tm = lambda w: pl.BlockSpec((1, w, ts), lambda bi, i: (bi, 0, i))
        in_specs = [nat(d)] + consts
        out_specs = [nat(dab), tm(dab), tm(dab), nat(dab), tm(dab), tm(dab), tm(n_forget)]
        sds = lambda shape, dt: jax.ShapeDtypeStruct(shape, dt)
        out_shape = (sds((b, s, dab), BF16), sds((b, dab, s), F32), sds((b, dab, s), F32),
                     sds((b, s, dab), BF16), sds((b, dab, s), F32), sds((b, dab, s), F32),
                     sds((b, n_forget, s), F32))
    else:
        n = h.shape[0]
        ts = _row_tile(n)
        grid = (n // ts,)
        row = lambda w: pl.BlockSpec((ts, w), lambda i: (i, 0))
        in_specs = [row(D_MODEL)] + consts
        out_specs = [row(dab)] * 6 + [row(LANES)]
        out_shape = tuple(jax.ShapeDtypeStruct((n, dab), dt) for dt in (BF16, F32, F32, BF16, F32, F32)) + (
            jax.ShapeDtypeStruct((n, LANES), F32),)
    return pl.pallas_call(
        functools.partial(_proj_ab_kernel, time_minor=time_minor),
        grid=grid,
        in_specs=in_specs,
        out_specs=out_specs,
        out_shape=out_shape,
        compiler_params=_params(len(grid), VMEM_LIMIT),
        name="proj_ab",
    )(h, g, w_main, wf, bf, gains)


def _proj_sb_kernel(h_ref, g_ref, w_ref, q_ref, k_ref, v_ref, *, time_minor):
    h = h_ref[0] if time_minor else h_ref[...]
    xn = _rms_rows(h, g_ref[...]).astype(BF16)
    dc = w_ref.shape[1] // 3
    q = jnp.dot(xn, w_ref[:, :dc], preferred_element_type=F32)
    q_dst = q_ref.at[0] if time_minor else q_ref
    q_dst[...] = (q * QK_SCALE).astype(BF16)
    for ref, c0 in ((k_ref, dc), (v_ref, 2 * dc)):
        y = jnp.dot(xn, w_ref[:, c0:c0 + dc], preferred_element_type=F32)
        for p in range(dc // LANES):
            cols = slice(p * LANES, (p + 1) * LANES)
            _store_rows(ref, cols, y[:, cols], time_minor)


def _proj_sb(h, g, w):
    time_minor = h.ndim == 3
    dc = w.shape[1] // 3
    consts = [_const_spec(g.shape), _const_spec(w.shape)]
    if time_minor:
        b, s, d = h.shape
        ts = _row_tile(s)
        grid = (b, s // ts)
        nat = lambda width: pl.BlockSpec((1, ts, width), lambda bi, i: (bi, i, 0))
        tm = pl.BlockSpec((1, dc, ts), lambda bi, i: (bi, 0, i))
        in_specs = [nat(d)] + consts
        out_specs = [nat(dc), tm, tm]
        out_shape = (jax.ShapeDtypeStruct((b, s, dc), BF16), jax.ShapeDtypeStruct((b, dc, s), F32),
                     jax.ShapeDtypeStruct((b, dc, s), F32))
    else:
        n = h.shape[0]
        ts = _row_tile(n)
        grid = (n // ts,)
        row = lambda width: pl.BlockSpec((ts, width), lambda i: (i, 0))
        in_specs = [row(D_MODEL)] + consts
        out_specs = [row(dc)] * 3
        out_shape = (jax.ShapeDtypeStruct((n, dc), BF16), jax.ShapeDtypeStruct((n, dc), F32),
                     jax.ShapeDtypeStruct((n, dc), F32))
    return pl.pallas_call(
        functools.partial(_proj_sb_kernel, time_minor=time_minor),
        grid=grid,
        in_specs=in_specs,
        out_specs=out_specs,
        out_shape=out_shape,
        compiler_params=_params(len(grid), VMEM_LIMIT),
        name="proj_sb",
    )(h, g, w)


def _split3(x):
    hi = x.astype(BF16)
    r = x - hi.astype(F32)
    mid = r.astype(BF16)
    lo = (r - mid.astype(F32)).astype(BF16)
    return hi, mid, lo


def _cumsum_kernel(x_ref, o_ref):
    rows, length = x_ref.shape
    r_i = lax.broadcasted_iota(jnp.int32, (LANES, LANES), 0)
    c_i = lax.broadcasted_iota(jnp.int32, (LANES, LANES), 1)
    tri = jnp.where(r_i <= c_i, 1.0, 0.0).astype(BF16)
    carry = jnp.zeros((rows, 1), F32)
    for b in range(length // LANES):
        hi, mid, lo = _split3(x_ref[:, b * LANES:(b + 1) * LANES])
        c = (jnp.dot(hi, tri, preferred_element_type=F32)
             + jnp.dot(mid, tri, preferred_element_type=F32)
             + jnp.dot(lo, tri, preferred_element_type=F32)) + carry
        o_ref[:, b * LANES:(b + 1) * LANES] = c
        carry = c[:, LANES - 1:LANES]


def _cumsum_rows(x):
    return pl.pallas_call(
        _cumsum_kernel,
        out_shape=jax.ShapeDtypeStruct(x.shape, F32),
        name="cumsum_logf",
    )(x)


def _prep_kv(k_ref, v_ref, kt, vb, time_minor):
    if time_minor:
        kt[...] = k_ref[0].astype(BF16)
        vb[...] = jnp.transpose(v_ref[0]).astype(BF16)
    else:
        kt[...] = jnp.transpose(k_ref[0]).astype(BF16)
        vb[...] = v_ref[0].astype(BF16)


def _stack_heads(q, lo):
    qf = q.astype(F32)
    return jnp.concatenate([jnp.where(lo, qf, 0.0), jnp.where(lo, 0.0, qf)], axis=0).astype(BF16)


def _unstack_heads(o2, lo):
    t = o2.shape[0] // 2
    return jnp.where(lo, o2[:t], o2[t:])


def _softmax_pv(pieces):
    m = None
    for x, _ in pieces:
        mx = jnp.max(x, axis=-1, keepdims=True)
        m = mx if m is None else jnp.maximum(m, mx)
    l, o = None, None
    for x, vv in pieces:
        p = jnp.exp(x - m)
        ls = jnp.sum(p, axis=-1, keepdims=True)
        os_ = jnp.dot(p.astype(BF16), vv, preferred_element_type=F32)
        l = ls if l is None else l + ls
        o = os_ if o is None else o + os_
    return o / l


def _stacked_causal(t):
    r = lax.broadcasted_iota(jnp.int32, (2 * t, t), 0)
    r = jnp.where(r >= t, r - t, r)
    c = lax.broadcasted_iota(jnp.int32, (2 * t, t), 1)
    return c <= r, c < r


def _attn_a_prompt_kernel(q_ref, k_ref, v_ref, bias_ref, o_ref, kt, vb, *, nq, win, ncase):
    lo = _lo_mask()
    _prep_kv(k_ref, v_ref, kt, vb, True)
    for i in range(nq):
        q0 = i * A_TQ
        case = min(i, ncase - 1)
        w0 = max(q0 + A_TQ - win, 0)
        wlen = min(win, q0 + A_TQ)
        q2 = _stack_heads(q_ref[0, q0:q0 + A_TQ, :], lo)
        x = jnp.dot(q2, kt[:, w0:w0 + wlen], preferred_element_type=F32) + bias_ref[0, case, :, :wlen]
        o2 = _softmax_pv([(x, vb[w0:w0 + wlen, :])])
        o_ref[0, q0:q0 + A_TQ, :] = _unstack_heads(o2, lo).astype(o_ref.dtype)


def _toeplitz_bias(rel_bias, rows, cols, shift):
    p = rows + cols - 1
    idx = np.clip(np.arange(p) - (rows - 1) + shift, -REL_CLIP, REL_CLIP) + REL_CLIP
    e = jnp.transpose(rel_bias[idx]).astype(F32)
    h = e.shape[0]
    skew = jnp.tile(e, (1, rows + 1))[:, :rows * (p + 1)].reshape(h, rows, p + 1)
    return skew[:, ::-1, :cols]


def _band_bias_prompt(rel_bias, s_len):
    win = min(A_WIN, s_len)
    nq = s_len // A_TQ
    ncase = min(nq, A_PAST // A_TQ + 1)
    r = np.arange(A_TQ)[:, None]
    j = np.arange(win)[None, :]
    tabs = []
    for c in range(ncase):
        q_off = c * A_TQ if (c + 1) * A_TQ <= win else win - A_TQ
        qc, kc = (q_off + r) // CHUNK, j // CHUNK
        ok = (kc <= qc) & (kc >= qc - A_LEFT_CHUNKS)
        tabs.append(jnp.where(ok[None], _toeplitz_bias(rel_bias, A_TQ, win, -q_off), NEG))
    tab = jnp.stack(tabs, axis=1)
    h = tab.shape[0]
    tab = jnp.transpose(tab.reshape(h // 2, 2, ncase, A_TQ, win), (0, 2, 1, 3, 4))
    return tab.reshape(h // 2, ncase, 2 * A_TQ, win), win, ncase


def _attn_a_prompt(q, k_t, v_t, rel_bias):
    b, s, da = q.shape
    pairs = da // LANES
    bias, win, ncase = _band_bias_prompt(rel_bias, s)
    nq = s // A_TQ
    blk = lambda: pl.BlockSpec((1, s, LANES), lambda bi, p: (bi, 0, p))
    blk_t = lambda: pl.BlockSpec((1, LANES, s), lambda bi, p: (bi, p, 0))
    return pl.pallas_call(
        functools.partial(_attn_a_prompt_kernel, nq=nq, win=win, ncase=ncase),
        grid=(b, pairs),
        in_specs=[blk(), blk_t(), blk_t(),
                  pl.BlockSpec((1, ncase, 2 * A_TQ, win), lambda bi, p: (p, 0, 0, 0))],
        out_specs=blk(),
        out_shape=jax.ShapeDtypeStruct((b, s, da), BF16),
        scratch_shapes=[pltpu.VMEM((LANES, s), BF16), pltpu.VMEM((s, LANES), BF16)],
        compiler_params=_params(2, VMEM_LIMIT),
        name="attn_a_prompt",
    )(q, k_t, v_t, bias)


def _attn_a_sample_kernel(q_ref, kc_ref, vc_ref, kn_ref, vn_ref, bc_ref, bn_ref, o_ref):
    lo = _lo_mask()
    q2 = _stack_heads(q_ref[0], lo)
    xc = jnp.dot(q2, kc_ref[0].astype(BF16), preferred_element_type=F32) + bc_ref[0]
    xn = _dot_nt(q2, kn_ref[0].astype(BF16)) + bn_ref[0]
    o2 = _softmax_pv([(xc, jnp.transpose(vc_ref[0]).astype(BF16)), (xn, vn_ref[0].astype(BF16))])
    o_ref[0] = _unstack_heads(o2, lo).astype(o_ref.dtype)


def _attn_a_sample(q, k_new, v_new, k_cache, v_cache, rel_bias, past):
    b, t, da = q.shape
    pairs = da // LANES
    n_band = k_cache.shape[2]
    q_pos = past + np.arange(t)[:, None]
    k_pos = past - n_band + np.arange(n_band + t)[None, :]
    qc, kc = q_pos // CHUNK, k_pos // CHUNK
    ok = (k_pos >= 0) & (kc <= qc) & (kc >= qc - A_LEFT_CHUNKS)
    tab = jnp.where(ok[None], _toeplitz_bias(rel_bias, t, n_band + t, -n_band), NEG)
    tab = tab.reshape(pairs, 2 * t, n_band + t)
    bias_c, bias_n = tab[..., :n_band], tab[..., n_band:]
    new = lambda: pl.BlockSpec((1, t, LANES), lambda bi, p: (bi, 0, p))
    old = lambda: pl.BlockSpec((1, LANES, n_band), lambda bi, p: (bi, p, 0))
    return pl.pallas_call(
        _attn_a_sample_kernel,
        grid=(b, pairs),
        in_specs=[new(), old(), old(), new(), new(),
                  pl.BlockSpec((1, 2 * t, n_band), lambda bi, p: (p, 0, 0)),
                  pl.BlockSpec((1, 2 * t, t), lambda bi, p: (p, 0, 0))],
        out_specs=new(),
        out_shape=jax.ShapeDtypeStruct((b, t, da), BF16),
        compiler_params=_params(2, VMEM_LIMIT),
        name="attn_a_sample",
    )(q, k_cache, v_cache, k_new, v_new, bias_c, bias_n)


def _attn_b_kernel(*refs, tq, nq, plen, time_minor):
    if plen:
        (q_ref, k_ref, v_ref, crow_ref, ccol_ref, kp_ref, vp_ref, cpre_ref, o_ref, kt, vb, kpt, vpb) = refs
    else:
        q_ref, k_ref, v_ref, crow_ref, ccol_ref, o_ref, kt, vb = refs
    pair = pl.program_id(1)
    lo = _lo_mask()
    _prep_kv(k_ref, v_ref, kt, vb, time_minor)
    if plen:
        _prep_kv(kp_ref, vp_ref, kpt, vpb, True)
    keep, _ = _stacked_causal(tq)
    head_lane = lax.broadcasted_iota(jnp.int32, (1, ccol_ref.shape[2]), 1)

    def logits(q2, kt_cols, cq, ck_ref, c0, c1):
        z = jnp.dot(q2, kt_cols, preferred_element_type=F32)
        return jnp.concatenate([z[:tq] + (cq[0] - ck_ref[0, 0, 0:1, c0:c1]),
                                z[tq:] + (cq[1] - ck_ref[0, 0, 1:2, c0:c1])], axis=0)

    for i in range(nq):
        d0, d1 = i * tq, (i + 1) * tq
        q2 = _stack_heads(q_ref[0, d0:d1, :], lo)
        cq_all = ccol_ref[0, d0:d1, :]
        cq = [jnp.sum(jnp.where(head_lane == 2 * pair + e, cq_all, 0.0), axis=-1, keepdims=True)
              for e in range(2)]
        pieces = []
        if plen:
            pieces.append((logits(q2, kpt[...], cq, cpre_ref, 0, plen), vpb[...]))
        if i:
            pieces.append((logits(q2, kt[:, 0:d0], cq, crow_ref, 0, d0), vb[0:d0, :]))
        xd = logits(q2, kt[:, d0:d1], cq, crow_ref, d0, d1)
        pieces.append((jnp.where(keep, xd, NEG), vb[d0:d1, :]))
        o_ref[0, d0:d1, :] = _unstack_heads(_softmax_pv(pieces), lo).astype(o_ref.dtype)


def _attn_b(q, k, v, c_self, k_pre=None, v_pre=None, c_pre=None, *, time_minor):
    b, t, db = q.shape
    pairs = db // LANES
    n_heads = c_self.shape[1]
    tq = min(ATT_T, t)
    c_col = jnp.transpose(c_self, (0, 2, 1))
    new = lambda: pl.BlockSpec((1, t, LANES), lambda bi, p: (bi, 0, p))
    tm = lambda n: pl.BlockSpec((1, LANES, n), lambda bi, p: (bi, p, 0))
    kv = (lambda: tm(t)) if time_minor else new
    in_specs = [new(), kv(), kv(),
                pl.BlockSpec((1, 1, 2, t), lambda bi, p: (bi, p, 0, 0)),
                pl.BlockSpec((1, t, n_heads), lambda bi, p: (bi, 0, 0))]
    args = [q, k, v, c_self.reshape(b, pairs, 2, t), c_col]
    scratch = [pltpu.VMEM((LANES, t), BF16), pltpu.VMEM((t, LANES), BF16)]
    plen = 0
    if k_pre is not None:
        plen = k_pre.shape[2]
        in_specs += [tm(plen), tm(plen), pl.BlockSpec((1, 1, 2, plen), lambda bi, p: (bi, p, 0, 0))]
        args += [k_pre, v_pre, c_pre.reshape(b, pairs, 2, plen)]
        scratch += [pltpu.VMEM((LANES, plen), BF16), pltpu.VMEM((plen, LANES), BF16)]
    return pl.pallas_call(
        functools.partial(_attn_b_kernel, tq=tq, nq=t // tq, plen=plen, time_minor=time_minor),
        grid=(b, pairs),
        in_specs=in_specs,
        out_specs=new(),
        out_shape=jax.ShapeDtypeStruct((b, t, db), BF16),
        scratch_shapes=scratch,
        compiler_params=_params(2, VMEM_LIMIT),
        name="attn_b",
    )(*args)


def _later_ones(t):
    r = lax.broadcasted_iota(jnp.int32, (t, t), 0)
    c = lax.broadcasted_iota(jnp.int32, (t, t), 1)
    return jnp.where(r > c, 1.0, 0.0).astype(BF16)


def _stick_weights(z, blk, ones_ref, carry, before):
    neg_abs = pltpu.bitcast(pltpu.bitcast(z, jnp.uint32) | jnp.uint32(0x80000000), F32)
    soft = jnp.maximum(z, 0.0) + jnp.log(1.0 + jnp.exp(neg_abs))
    nblk = z.shape[1] // blk
    ws = [None] * nblk
    for j in reversed(range(nblk)):
        cols = slice(j * blk, (j + 1) * blk)
        masked = before is not None and j == nblk - 1
        sp = soft[:, cols]
        keep = jnp.where(before, sp, 0.0) if masked else sp
        later = jnp.dot(keep.astype(BF16), ones_ref[...], preferred_element_type=F32)
        w = jnp.exp((z[:, cols] - sp) - (later + carry))
        if masked:
            w = jnp.where(before, w, 0.0)
        ws[j] = w.astype(BF16)
        carry = carry + (later[:, 0:1] + keep[:, 0:1])
    return (ws[0] if nblk == 1 else jnp.concatenate(ws, axis=-1)), carry


def _attn_c_kernel(*refs, tq, nq, plen, tpre, time_minor):
    if plen:
        q_ref, k_ref, v_ref, kp_ref, vp_ref, o_ref, kt, vb, u_self, kpt, vpb, u_pre = refs
    else:
        q_ref, k_ref, v_ref, o_ref, kt, vb, u_self = refs
    lo = _lo_mask()
    _prep_kv(k_ref, v_ref, kt, vb, time_minor)
    u_self[...] = _later_ones(tq)
    if plen:
        _prep_kv(kp_ref, vp_ref, kpt, vpb, True)
        u_pre[...] = _later_ones(tpre)
    _, before = _stacked_causal(tq)
    for i in range(nq):
        d0, d1 = i * tq, (i + 1) * tq
        q2 = _stack_heads(q_ref[0, d0:d1, :], lo)
        z = jnp.dot(q2, kt[:, 0:d1], preferred_element_type=F32)
        w, carry = _stick_weights(z, tq, u_self, jnp.zeros((2 * tq, 1), F32), before)
        o2 = jnp.dot(w, vb[0:d1, :], preferred_element_type=F32)
        if plen:
            wp, _ = _stick_weights(jnp.dot(q2, kpt[...], preferred_element_type=F32), tpre, u_pre, carry, None)
            o2 = o2 + jnp.dot(wp, vpb[...], preferred_element_type=F32)
        o_ref[0, d0:d1, :] = _unstack_heads(o2, lo).astype(o_ref.dtype)


def _attn_c(q, k, v, k_pre=None, v_pre=None, *, time_minor):
    b, t, dc = q.shape
    pairs = dc // LANES
    tq = min(ATT_T, t)
    new = lambda: pl.BlockSpec((1, t, LANES), lambda bi, p: (bi, 0, p))
    tm = lambda n: pl.BlockSpec((1, LANES, n), lambda bi, p: (bi, p, 0))
    kv = (lambda: tm(t)) if time_minor else new
    in_specs = [new(), kv(), kv()]
    args = [q, k, v]
    scratch = [pltpu.VMEM((LANES, t), BF16), pltpu.VMEM((t, LANES), BF16), pltpu.VMEM((tq, tq), BF16)]
    plen = tpre = 0
    if k_pre is not None:
        plen = k_pre.shape[2]
        tpre = min(ATT_T, plen)
        in_specs += [tm(plen), tm(plen)]
        args += [k_pre, v_pre]
        scratch += [pltpu.VMEM((LANES, plen), BF16), pltpu.VMEM((plen, LANES), BF16),
                    pltpu.VMEM((tpre, tpre), BF16)]
    return pl.pallas_call(
        functools.partial(_attn_c_kernel, tq=tq, nq=t // tq, plen=plen, tpre=tpre, time_minor=time_minor),
        grid=(b, pairs),
        in_specs=in_specs,
        out_specs=new(),
        out_shape=jax.ShapeDtypeStruct((b, t, dc), BF16),
        scratch_shapes=scratch,
        compiler_params=_params(2, VMEM_LIMIT),
        name="attn_c",
    )(*args)


def _softmax_rows(x):
    m = jnp.max(x, axis=-1, keepdims=True)
    p = jnp.exp(x - m)
    return p, jnp.sum(p, axis=-1, keepdims=True)


def _head_norm_x(y, gain):
    parts = []
    for h in range(N_HEADS_X):
        blk = y[:, h * HEAD_DIM_X:(h + 1) * HEAD_DIM_X]
        ms = jnp.mean(blk * blk, axis=-1, keepdims=True)
        parts.append(blk * lax.rsqrt(ms + EPS) * gain)
    return parts


def _mem_kv_kernel(m_ref, g_ref, w_ref, gain_ref, k_ref, v_ref, kb_ref, vb_ref):
    xn = _rms_rows(m_ref[...], g_ref[0]).astype(BF16)
    dx = w_ref.shape[2] // 2
    k = jnp.dot(xn, w_ref[0, :, :dx], preferred_element_type=F32)
    for h, blk in enumerate(_head_norm_x(k, gain_ref[0])):
        k_ref[0, :, h * HEAD_DIM_X:(h + 1) * HEAD_DIM_X] = blk
        kb_ref[0, :, h * HEAD_DIM_X:(h + 1) * HEAD_DIM_X] = blk.astype(BF16)
    v = jnp.dot(xn, w_ref[0, :, dx:], preferred_element_type=F32)
    v_ref[0] = v
    vb_ref[0] = v.astype(BF16)


def _mem_kv(mem2d, g, w_kv, k_gain):
    n = mem2d.shape[0]
    depth = w_kv.shape[0]
    ts = _row_tile(n)
    dx = w_kv.shape[2] // 2
    per_layer = lambda a: pl.BlockSpec((1,) + a.shape[1:], lambda l, i: (l, 0, 0))
    out = pl.BlockSpec((1, ts, dx), lambda l, i: (l, i, 0))
    return pl.pallas_call(
        _mem_kv_kernel,
        grid=(depth, n // ts),
        in_specs=[pl.BlockSpec((ts, D_MODEL), lambda l, i: (i, 0)), per_layer(g), per_layer(w_kv), per_layer(k_gain)],
        out_specs=[out] * 4,
        out_shape=tuple(jax.ShapeDtypeStruct((depth, n, dx), dt) for dt in (F32, F32, BF16, BF16)),
        compiler_params=_params(2, VMEM_LIMIT),
        name="mem_kv",
    )(mem2d, g, w_kv, k_gain)


def _post_kernel(h_ref, o1_ref, o2_ref, wout_ref, g_ref, wq_ref, qgain_ref, mk_ref, mv_ref, wo_ref, y_ref):
    half = o1_ref.shape[2]
    h1 = (h_ref[0]
          + jnp.dot(o1_ref[0], wout_ref[:half, :], preferred_element_type=F32)
          + jnp.dot(o2_ref[0], wout_ref[half:, :], preferred_element_type=F32))
    hn = _rms_rows(h1, g_ref[...]).astype(BF16)
    qx = jnp.dot(hn, wq_ref[...], preferred_element_type=F32)
    heads = []
    for h, qh in enumerate(_head_norm_x(qx, qgain_ref[...])):
        cols = slice(h * HEAD_DIM_X, (h + 1) * HEAD_DIM_X)
        s = _dot_nt((qh * X_SCALE).astype(BF16), mk_ref[0, :, cols])
        p, l = _softmax_rows(s)
        oh = jnp.dot(p.astype(BF16), mv_ref[0, :, cols], preferred_element_type=F32) / l
        heads.append(oh.astype(BF16))
    y_ref[0] = h1 + jnp.dot(jnp.concatenate(heads, axis=-1), wo_ref[...], preferred_element_type=F32)


def _post(h, o1, o2, w_out, g_cross, w_q, q_gain, mk, mv, w_o):
    b, s, d = h.shape
    ts = _row_tile(s)
    half = w_out.shape[0] // 2
    n_mem = mk.shape[1]
    tile = lambda width: pl.BlockSpec((1, ts, width), lambda bi, i: (bi, i, 0))
    o1_spec = pl.BlockSpec((1, ts, half), lambda bi, i: (bi, i, 0))
    if o2 is None:
        o2 = o1
        o2_spec = pl.BlockSpec((1, ts, half), lambda bi, i: (bi, i, 1))
    else:
        o2_spec = o1_spec
    mem = lambda: pl.BlockSpec((1, n_mem, d), lambda bi, i: (bi, 0, 0))
    return pl.pallas_call(
        _post_kernel,
        grid=(b, s // ts),
        in_specs=[tile(d), o1_spec, o2_spec, _const_spec(w_out.shape), _const_spec(g_cross.shape),
                  _const_spec(w_q.shape), _const_spec(q_gain.shape), mem(), mem(), _const_spec(w_o.shape)],
        out_specs=tile(d),
        out_shape=jax.ShapeDtypeStruct((b, s, d), F32),
        compiler_params=_params(2, VMEM_LIMIT),
        name="post_mixer_cross",
    )(h, o1, o2, w_out, g_cross, w_q, q_gain, mk, mv, w_o)


def _mlp_kernel(h_ref, g_ref, wup_ref, wdn_ref, y_ref):
    h = h_ref[...]
    hn = _rms_rows(h, g_ref[...]).astype(BF16)
    acc = h
    for c in range(wup_ref.shape[1] // FF_CHUNK):
        cols = slice(c * FF_CHUNK, (c + 1) * FF_CHUNK)
        u = jnp.maximum(jnp.dot(hn, wup_ref[:, cols], preferred_element_type=F32), 0.0)
        acc = acc + jnp.dot((u * u).astype(BF16), wdn_ref[cols, :], preferred_element_type=F32)
    y_ref[...] = acc


def _mlp(h2d, g, w_up, w_down):
    n = h2d.shape[0]
    ts = _row_tile(n)
    row = pl.BlockSpec((ts, D_MODEL), lambda i: (i, 0))
    return pl.pallas_call(
        _mlp_kernel,
        grid=(n // ts,),
        in_specs=[row, _const_spec(g.shape), _const_spec(w_up.shape), _const_spec(w_down.shape)],
        out_specs=row,
        out_shape=jax.ShapeDtypeStruct((n, D_MODEL), F32),
        compiler_params=_params(1, VMEM_LIMIT),
        name="mlp",
    )(h2d, g, w_up, w_down)


def _row_vec(x):
    return x.reshape(1, -1).astype(F32)


def _ab_weights(w_in, b_f, qk_gain):
    dab3 = w_in.shape[1] - b_f.shape[0]
    n_f = b_f.shape[0]
    w_main = w_in[:, :dab3].astype(BF16)
    wf = jnp.zeros((w_in.shape[0], LANES), F32).at[:, :n_f].set(w_in[:, dab3:]).astype(BF16)
    bf = jnp.zeros((1, LANES), F32).at[0, :n_f].set(b_f)
    gains = jnp.concatenate([qk_gain, qk_gain], axis=-1).astype(F32)
    return w_main, wf, bf, gains


def _time_minor(x):
    b, t = x.shape[:2]
    return jnp.transpose(x.reshape(b, t, -1), (0, 2, 1))


def _heads_out(x_t, n_heads):
    b, _, t = x_t.shape
    return jnp.transpose(x_t, (0, 2, 1)).reshape(b, t, n_heads, HEAD_DIM)


def kernel(x_prompt, x_sample, mem_prompt, cache_a_k, cache_a_v, cache_b_k, cache_b_v, cache_b_logf,
           cache_c_k, cache_c_v, cache_mem_k, cache_mem_v, norm_mix, norm_cross, norm_mlp, norm_mem,
           ab_w_in, ab_forget_bias, ab_qk_gain, ab_rel_bias, ab_w_out, sb_w_in, sb_w_out,
           x_w_q, x_w_kv, x_qk_gain, x_w_o, mlp_w_up, mlp_w_down):
    bp, sp, d = x_prompt.shape
    bs, ss, _ = x_sample.shape
    depth = norm_mix.shape[0]
    n_mem = mem_prompt.shape[1]
    hb = ab_forget_bias.shape[1]
    hp, hs = x_prompt, x_sample
    outs = {name: [] for name in ("a_kp", "a_vp", "a_ks", "a_vs", "b_kp", "b_vp", "b_fp", "b_ks", "b_vs",
                                  "b_fs", "c_kp", "c_vp", "c_ks", "c_vs", "m_kp", "m_vp")}
    mk_all, mv_all, mkb_all, mvb_all = _mem_kv(
        mem_prompt.reshape(bp * n_mem, d), norm_mem.reshape(depth, 1, d).astype(F32), x_w_kv.astype(BF16),
        x_qk_gain[:, 1].reshape(depth, 1, -1).astype(F32))

    for layer in range(depth):
        i = layer // 2
        g_mix = _row_vec(norm_mix[layer])
        if layer % 2 == 0:
            w_main, wf, bf, gains = _ab_weights(ab_w_in[i], ab_forget_bias[i], ab_qk_gain[i])
            da = w_main.shape[1] // 6
            ha = da // HEAD_DIM
            qa, ka, va, qb, kb, vb, lf = _proj_ab(hp, g_mix, w_main, wf, bf, gains, hb)
            c_p = _cumsum_rows(lf.reshape(bp * hb, sp)).reshape(bp, hb, sp)
            oa = _attn_a_prompt(qa, ka, va, ab_rel_bias[i])
            ob = _attn_b(qb, kb, vb, c_p, time_minor=True)
            keep = min(A_PAST, sp)
            outs["a_kp"].append(_heads_out(ka[:, :, sp - keep:], ha))
            outs["a_vp"].append(_heads_out(va[:, :, sp - keep:], ha))
            outs["b_kp"].append(_heads_out(kb, hb))
            outs["b_vp"].append(_heads_out(vb, hb))
            outs["b_fp"].append(jnp.transpose(lf, (0, 2, 1)))
            mix_p = (oa, ob)
            qa, ka, va, qb, kb, vb, lf = _proj_ab(hs.reshape(bs * ss, d), g_mix, w_main, wf, bf, gains, hb)
            r3 = lambda x: x.reshape(bs, ss, x.shape[-1])
            qa, ka, va, qb, kb, vb = (r3(x) for x in (qa, ka, va, qb, kb, vb))
            logf = r3(lf)[..., :hb]
            past = cache_b_k.shape[2]
            heads_first = lambda x: jnp.transpose(x, (0, 2, 1))
            lf_all = jnp.concatenate([heads_first(cache_b_logf[i].astype(F32)), heads_first(logf)], axis=-1)
            total = past + ss
            padded = -(-total // LANES) * LANES
            lf_all = jnp.pad(lf_all, ((0, 0), (0, 0), (0, padded - total)))
            c_all = _cumsum_rows(lf_all.reshape(bs * hb, padded)).reshape(bs, hb, padded)
            oa = _attn_a_sample(qa, ka, va, _time_minor(cache_a_k[i]), _time_minor(cache_a_v[i]),
                                ab_rel_bias[i], past)
            ob = _attn_b(qb, kb, vb, c_all[..., past:total], _time_minor(cache_b_k[i]), _time_minor(cache_b_v[i]),
                         c_all[..., :past], time_minor=False)
            outs["a_ks"].append(ka.reshape(bs, ss, ha, HEAD_DIM))
            outs["a_vs"].append(va.reshape(bs, ss, ha, HEAD_DIM))
            outs["b_ks"].append(kb.reshape(bs, ss, hb, HEAD_DIM))
            outs["b_vs"].append(vb.reshape(bs, ss, hb, HEAD_DIM))
            outs["b_fs"].append(logf)
            mix_s = (oa, ob)
            w_out = ab_w_out[i].astype(BF16)
        else:
            w_sb = sb_w_in[i].astype(BF16)
            dc = w_sb.shape[1] // 3
            hc = dc // HEAD_DIM
            q, k, v = _proj_sb(hp, g_mix, w_sb)
            mix_p = (_attn_c(q, k, v, time_minor=True), None)
            outs["c_kp"].append(_heads_out(k, hc))
            outs["c_vp"].append(_heads_out(v, hc))
            q, k, v = _proj_sb(hs.reshape(bs * ss, d), g_mix, w_sb)
            q, k, v = (x.reshape(bs, ss, dc) for x in (q, k, v))
            mix_s = (_attn_c(q, k, v, _time_minor(cache_c_k[i]), _time_minor(cache_c_v[i]), time_minor=False),
                     None)
            outs["c_ks"].append(k.reshape(bs, ss, hc, HEAD_DIM))
            outs["c_vs"].append(v.reshape(bs, ss, hc, HEAD_DIM))
            w_out = sb_w_out[i].astype(BF16)

        mkb, mvb = mkb_all[layer], mvb_all[layer]
        g_cross = _row_vec(norm_cross[layer])
        w_q, w_o = x_w_q[layer].astype(BF16), x_w_o[layer].astype(BF16)
        q_gain = _row_vec(x_qk_gain[layer, 0])
        hp = _post(hp, mix_p[0], mix_p[1], w_out, g_cross, w_q, q_gain,
                   mkb.reshape(bp, n_mem, d), mvb.reshape(bp, n_mem, d), w_o)
        hs = _post(hs, mix_s[0], mix_s[1], w_out, g_cross, w_q, q_gain,
                   cache_mem_k[layer].reshape(bs, n_mem, d).astype(BF16),
                   cache_mem_v[layer].reshape(bs, n_mem, d).astype(BF16), w_o)
        g_mlp = _row_vec(norm_mlp[layer])
        w_up, w_dn = mlp_w_up[layer].astype(BF16), mlp_w_down[layer].astype(BF16)
        hp = _mlp(hp.reshape(bp * sp, d), g_mlp, w_up, w_dn).reshape(bp, sp, d)
        hs = _mlp(hs.reshape(bs * ss, d), g_mlp, w_up, w_dn).reshape(bs, ss, d)

    st = lambda name: jnp.stack(outs[name])
    mem_shape = (depth, bp, n_mem, N_HEADS_X, HEAD_DIM_X)
    return (hp, hs, st("a_kp"), st("a_vp"), st("a_ks"), st("a_vs"),
            st("b_kp"), st("b_vp"), st("b_fp"), st("b_ks"), st("b_vs"), st("b_fs"),
            st("c_kp"), st("c_vp"), st("c_ks"), st("c_vs"), mk_all.reshape(mem_shape), mv_all.reshape(mem_shape))
```
